```python
import jax, jax.numpy as jnp
from jax import lax
import numpy as np

D_MODEL = 2048
BATCH = 4
SEQ = 2048
DEPTH = 1

HEAD_DIM = 128
N_HEADS_TOTAL = D_MODEL // HEAD_DIM
N_HEADS_MOBA = N_HEADS_TOTAL // 2
N_HEADS_SB = N_HEADS_TOTAL - N_HEADS_MOBA
W_MOBA = N_HEADS_MOBA * HEAD_DIM
W_SB = N_HEADS_SB * HEAD_DIM
D_MIX = W_MOBA + W_SB
D_FF = -(-8 * D_MODEL // (3 * 256)) * 256
MOBA_BLOCK = 256
MOBA_TOPK = 3
MOBA_Q_CHUNK = 32
SB_Q_BLOCK = 128
ROPE_THETA = 500000.0
ROT_DIM = HEAD_DIM // 4
RMS_EPS = 1e-6
FFN_RES_SCALE = 0.5

kernel_name = "hybrid_moba_stickbreaking_macaron_block"


def rmsnorm(x, g):
    xf = x.astype(jnp.float32)
    y = xf * lax.rsqrt(jnp.mean(xf * xf, axis=-1, keepdims=True) + RMS_EPS) * g.astype(jnp.float32)
    return y.astype(x.dtype)


def swiglu(x, w_gate, w_up, w_down):
    return (jax.nn.silu(x @ w_gate) * (x @ w_up)) @ w_down


def partial_rope(x):
    S = x.shape[2]
    half = ROT_DIM // 2
    inv_freq = ROPE_THETA ** (-jnp.arange(0, ROT_DIM, 2, dtype=jnp.float32) / ROT_DIM)
    ang = jnp.arange(S, dtype=jnp.float32)[:, None] * inv_freq[None, :]
    cos, sin = jnp.cos(ang), jnp.sin(ang)
    xr = x[..., :ROT_DIM].astype(jnp.float32)
    x1, x2 = xr[..., :half], xr[..., half:]
    rot = jnp.concatenate([x1 * cos - x2 * sin, x2 * cos + x1 * sin], axis=-1).astype(x.dtype)
    return jnp.concatenate([rot, x[..., ROT_DIM:]], axis=-1)


def moba_attention(q, k, v):
    B, H, S, D = q.shape
    nb = -(-S // MOBA_BLOCK)
    pad = nb * MOBA_BLOCK - S
    kp = jnp.pad(k, ((0, 0), (0, 0), (0, pad), (0, 0)))
    vp = jnp.pad(v, ((0, 0), (0, 0), (0, pad), (0, 0)))
    k_blocks = kp.reshape(B, H, nb, MOBA_BLOCK, D)
    v_blocks = vp.reshape(B, H, nb, MOBA_BLOCK, D)
    k_mean = jnp.mean(k_blocks.astype(jnp.float32), axis=3)
    topk = min(MOBA_TOPK, nb - 1)
    scale = D ** -0.5
    n_chunks = S // MOBA_Q_CHUNK
    q_chunks = q.reshape(B, H, n_chunks, MOBA_Q_CHUNK, D).transpose(2, 0, 1, 3, 4)
    b_idx = jnp.arange(B)[:, None, None, None]
    h_idx = jnp.arange(H)[None, :, None, None]
    block_ids = jnp.arange(nb)
    offs = jnp.arange(MOBA_BLOCK)

    def chunk_fn(args):
        c, qc = args
        t0 = c * MOBA_Q_CHUNK
        q_pos = t0 + jnp.arange(MOBA_Q_CHUNK)
        own = t0 // MOBA_BLOCK
        qf = qc.astype(jnp.float32)
        k_own = lax.dynamic_index_in_dim(k_blocks, own, axis=2, keepdims=False)
        v_own = lax.dynamic_index_in_dim(v_blocks, own, axis=2, keepdims=False)
        s_own = jnp.einsum('bhqd,bhkd->bhqk', qf, k_own.astype(jnp.float32)) * scale
        own_pos = own * MOBA_BLOCK + offs
        s_own = jnp.where(own_pos[None, :] <= q_pos[:, None], s_own, -jnp.inf)
        if topk > 0:
            gate = jnp.einsum('bhqd,bhnd->bhqn', qf, k_mean)
            gate = jnp.where(block_ids < own, gate, -jnp.inf)
            _, sel = lax.top_k(gate, topk)
            sel_valid = sel < own
            k_sel = k_blocks[b_idx, h_idx, sel]
            v_sel = v_blocks[b_idx, h_idx, sel]
            s_sel = jnp.einsum('bhqd,bhqnkd->bhqnk', qf, k_sel.astype(jnp.float32)) * scale
            s_sel = jnp.where(sel_valid[..., None], s_sel, -jnp.inf)
            s_sel = s_sel.reshape(B, H, MOBA_Q_CHUNK, topk * MOBA_BLOCK)
            p = jax.nn.softmax(jnp.concatenate([s_sel, s_own], axis=-1), axis=-1)
            p_sel = p[..., :topk * MOBA_BLOCK].reshape(B, H, MOBA_Q_CHUNK, topk, MOBA_BLOCK)
            p_own = p[..., topk * MOBA_BLOCK:]
            out = (jnp.einsum('bhqnk,bhqnkd->bhqd', p_sel.astype(v.dtype), v_sel)
                   + jnp.einsum('bhqk,bhkd->bhqd', p_own.astype(v.dtype), v_own))
        else:
            p_own = jax.nn.softmax(s_own, axis=-1)
            out = jnp.einsum('bhqk,bhkd->bhqd', p_own.astype(v.dtype), v_own)
        return out

    outs = lax.map(chunk_fn, (jnp.arange(n_chunks), q_chunks))
    return outs.transpose(1, 2, 0, 3, 4).reshape(B, H, S, D)


def stick_breaking_attention(q, k, v):
    B, H, S, D = q.shape
    scale = D ** -0.5
    nqb = S // SB_Q_BLOCK
    kf = k.astype(jnp.float32)
    k_pos = jnp.arange(S)
    q_blocks = q.reshape(B, H, nqb, SB_Q_BLOCK, D).transpose(2, 0, 1, 3, 4)

    def block_fn(args):
        i, qb = args
        q_pos = i * SB_Q_BLOCK + jnp.arange(SB_Q_BLOCK)
        z = jnp.einsum('bhqd,bhkd->bhqk', qb.astype(jnp.float32), kf) * scale
        causal = k_pos[None, :] < q_pos[:, None]
        log_beta = jax.nn.log_sigmoid(z)
        log_one_minus = jnp.where(causal, jax.nn.log_sigmoid(-z), 0.0)
        log_stick = lax.cumsum(log_one_minus, axis=3, reverse=True) - log_one_minus
        a = jnp.where(causal, jnp.exp(log_beta + log_stick), 0.0)
        return jnp.einsum('bhqk,bhkd->bhqd', a.astype(v.dtype), v)

    outs = lax.map(block_fn, (jnp.arange(nqb), q_blocks))
    return outs.transpose(1, 2, 0, 3, 4).reshape(B, H, S, D)


def split_heads(t, n_heads):
    B, S, _ = t.shape
    return t.reshape(B, S, n_heads, HEAD_DIM).transpose(0, 2, 1, 3)


def merge_heads(t):
    B, H, S, D = t.shape
    return t.transpose(0, 2, 1, 3).reshape(B, S, H * D)


def setup_inputs(seed: int = 0) -> dict:
    key = jax.random.key(seed)
    ks = jax.random.split(key, 20)

    def w(k, shape, fan_in):
        return jax.random.normal(k, shape, jnp.float32) * (fan_in ** -0.5)

    def gain(k, n):
        return 1.0 + 0.02 * jax.random.normal(k, (DEPTH, n), jnp.float32)

    return {
        "x": jax.random.normal(ks[0], (BATCH, SEQ, D_MODEL), jnp.float32),
        "ffn1_pre_g": gain(ks[1], D_MODEL),
        "ffn1_w_gate": w(ks[2], (DEPTH, D_MODEL, D_FF), D_MODEL),
        "ffn1_w_up": w(ks[3], (DEPTH, D_MODEL, D_FF), D_MODEL),
        "ffn1_w_down": w(ks[4], (DEPTH, D_FF, D_MODEL), D_FF),
        "ffn1_post_g": gain(ks[5], D_MODEL),
        "mix_pre_g": gain(ks[6], D_MODEL),
        "w_in": w(ks[7], (DEPTH, D_MODEL, 3 * D_MIX), D_MODEL),
        "moba_out_g": gain(ks[8], W_MOBA),
        "sb_out_g": gain(ks[9], W_SB),
        "w_out": w(ks[10], (DEPTH, D_MIX, D_MODEL), D_MIX),
        "mix_post_g": gain(ks[11], D_MODEL),
        "ffn2_pre_g": gain(ks[12], D_MODEL),
        "ffn2_w_gate": w(ks[13], (DEPTH, D_MODEL, D_FF), D_MODEL),
        "ffn2_w_up": w(ks[14], (DEPTH, D_MODEL, D_FF), D_MODEL),
        "ffn2_w_down": w(ks[15], (DEPTH, D_FF, D_MODEL), D_FF),
        "ffn2_post_g": gain(ks[16], D_MODEL),
    }


def reference(x, ffn1_pre_g, ffn1_w_gate, ffn1_w_up, ffn1_w_down, ffn1_post_g,
              mix_pre_g, w_in, moba_out_g, sb_out_g, w_out, mix_post_g,
              ffn2_pre_g, ffn2_w_gate, ffn2_w_up, ffn2_w_down, ffn2_post_g):
    for l in range(DEPTH):
        f = swiglu(rmsnorm(x, ffn1_pre_g[l]), ffn1_w_gate[l], ffn1_w_up[l], ffn1_w_down[l])
        x = x + FFN_RES_SCALE * rmsnorm(f, ffn1_post_g[l])

        h = rmsnorm(x, mix_pre_g[l])
        proj = h @ w_in[l]
        q_a, k_a, v_a, q_b, k_b, v_b = jnp.split(
            proj, [W_MOBA, 2 * W_MOBA, 3 * W_MOBA, 3 * W_MOBA + W_SB, 3 * W_MOBA + 2 * W_SB], axis=-1)
        q_a = partial_rope(split_heads(q_a, N_HEADS_MOBA))
        k_a = partial_rope(split_heads(k_a, N_HEADS_MOBA))
        o_a = moba_attention(q_a, k_a, split_heads(v_a, N_HEADS_MOBA))
        o_b = stick_breaking_attention(split_heads(q_b, N_HEADS_SB), split_heads(k_b, N_HEADS_SB),
                                       split_heads(v_b, N_HEADS_SB))
        o = jnp.concatenate([rmsnorm(merge_heads(o_a), moba_out_g[l]),
                             rmsnorm(merge_heads(o_b), sb_out_g[l])], axis=-1)
        x = x + rmsnorm(o @ w_out[l], mix_post_g[l])

        f = swiglu(rmsnorm(x, ffn2_pre_g[l]), ffn2_w_gate[l], ffn2_w_up[l], ffn2_w_down[l])
        x = x + FFN_RES_SCALE * rmsnorm(f, ffn2_post_g[l])
    return x
```

```python
import functools

import jax
import jax.numpy as jnp
from jax import lax
from jax.experimental import pallas as pl
from jax.experimental.pallas import tpu as pltpu

HEAD_DIM = 128
MOBA_BLOCK = 256
MOBA_TOPK = 3
ROPE_THETA = 500000.0
ROT_DIM = HEAD_DIM // 4
RMS_EPS = 1e-6
FFN_RES_SCALE = 0.5

LANES = 128
SB_BLOCK = 256
VMEM_LIMIT = 56 * 1024 * 1024

FFN_TM, FFN_TF = 512, 512
PROJ_TM, PROJ_TN = 1024, 512
OUT_TM = 512


def _rms(xf, g):
    return xf * lax.rsqrt(jnp.mean(xf * xf, axis=-1, keepdims=True) + RMS_EPS) * g


def _dot(a, b):
    return jnp.dot(a, b, preferred_element_type=jnp.float32)


def _dot_nt(a, b):
    return lax.dot_general(a, b, (((1,), (1,)), ((), ())), preferred_element_type=jnp.float32)


def _ffn_kernel(x_ref, pre_g_ref, wg_ref, wu_ref, wd_ref, post_g_ref, o_ref, xn_ref, acc_ref):
    j = pl.program_id(1)

    @pl.when(j == 0)
    def _():
        xn_ref[...] = _rms(x_ref[...], pre_g_ref[...]).astype(xn_ref.dtype)
        acc_ref[...] = jnp.zeros_like(acc_ref)

    xn = xn_ref[...]
    g = _dot(xn, wg_ref[...])
    u = _dot(xn, wu_ref[...])
    h = (g * jax.nn.sigmoid(g) * u).astype(jnp.bfloat16)
    acc_ref[...] += _dot(h, wd_ref[...])

    @pl.when(j == pl.num_programs(1) - 1)
    def _():
        o_ref[...] = x_ref[...] + FFN_RES_SCALE * _rms(acc_ref[...], post_g_ref[...])


def _ffn(x, pre_g, wg, wu, wd, post_g):
    T, D = x.shape
    F = wg.shape[1]
    tm, tf = FFN_TM, FFN_TF
    assert T % tm == 0 and F % tf == 0
    return pl.pallas_call(
        _ffn_kernel,
        grid=(T // tm, F // tf),
        in_specs=[
            pl.BlockSpec((tm, D), lambda i, j: (i, 0)),
            pl.BlockSpec((1, D), lambda i, j: (0, 0)),
            pl.BlockSpec((D, tf), lambda i, j: (0, j)),
            pl.BlockSpec((D, tf), lambda i, j: (0, j)),
            pl.BlockSpec((tf, D), lambda i, j: (j, 0)),
            pl.BlockSpec((1, D), lambda i, j: (0, 0)),
        ],
        out_specs=pl.BlockSpec((tm, D), lambda i, j: (i, 0)),
        out_shape=jax.ShapeDtypeStruct((T, D), jnp.float32),
        scratch_shapes=[pltpu.VMEM((tm, D), jnp.bfloat16), pltpu.VMEM((tm, D), jnp.float32)],
        compiler_params=pltpu.CompilerParams(
            dimension_semantics=("parallel", "arbitrary"), vmem_limit_bytes=VMEM_LIMIT),
        name="ffn",
    )(x, pre_g, wg, wu, wd, post_g)


def _proj_kernel(n_rope_tiles, x_ref, g_ref, w_ref, cos_ref, sa_ref, sb_ref, o_ref, xn_ref):
    j = pl.program_id(1)

    @pl.when(j == 0)
    def _():
        xn_ref[...] = _rms(x_ref[...], g_ref[...]).astype(xn_ref.dtype)

    y = _dot(xn_ref[...], w_ref[...])

    @pl.when(j < n_rope_tiles)
    def _():
        c, sa, sb = cos_ref[...], sa_ref[...], sb_ref[...]
        for h in range(y.shape[1] // HEAD_DIM):
            yh = y[:, h * HEAD_DIM:(h + 1) * HEAD_DIM]
            r = (yh * c + pltpu.roll(yh, HEAD_DIM - ROT_DIM // 2, 1) * sa
                 + pltpu.roll(yh, ROT_DIM // 2, 1) * sb)
            o_ref[:, h * HEAD_DIM:(h + 1) * HEAD_DIM] = r.astype(o_ref.dtype)

    @pl.when(j >= n_rope_tiles)
    def _():
        o_ref[...] = y.astype(o_ref.dtype)


def _proj(x, g, w, cos_t, sa_t, sb_t, seq, n_rope_cols):
    T, D = x.shape
    N = w.shape[1]
    tm, tn = PROJ_TM, PROJ_TN
    assert T % tm == 0 and N % tn == 0 and seq % tm == 0 and n_rope_cols % tn == 0
    pos_tiles = seq // tm
    tab_spec = pl.BlockSpec((tm, HEAD_DIM), lambda i, j: (i % pos_tiles, 0))
    return pl.pallas_call(
        functools.partial(_proj_kernel, n_rope_cols // tn),
        grid=(T // tm, N // tn),
        in_specs=[
            pl.BlockSpec((tm, D), lambda i, j: (i, 0)),
            pl.BlockSpec((1, D), lambda i, j: (0, 0)),
            pl.BlockSpec((D, tn), lambda i, j: (0, j)),
            tab_spec, tab_spec, tab_spec,
        ],
        out_specs=pl.BlockSpec((tm, tn), lambda i, j: (i, j)),
        out_shape=jax.ShapeDtypeStruct((T, N), jnp.bfloat16),
        scratch_shapes=[pltpu.VMEM((tm, D), jnp.bfloat16)],
        compiler_params=pltpu.CompilerParams(
            dimension_semantics=("parallel", "arbitrary"), vmem_limit_bytes=VMEM_LIMIT),
        name="proj",
    )(x, g, w, cos_t, sa_t, sb_t)


def _rope_tables(seq):
    half = ROT_DIM // 2
    inv_freq = ROPE_THETA ** (-jnp.arange(0, ROT_DIM, 2, dtype=jnp.float32) / ROT_DIM)
    ang = jnp.arange(seq, dtype=jnp.float32)[:, None] * inv_freq[None, :]
    cos, sin = jnp.cos(ang), jnp.sin(ang)
    zeros = jnp.zeros((seq, HEAD_DIM - half), jnp.float32)
    cos_t = jnp.concatenate([cos, cos, jnp.ones((seq, HEAD_DIM - ROT_DIM), jnp.float32)], axis=1)
    sa_t = jnp.concatenate([-sin, zeros], axis=1)
    sb_t = jnp.concatenate([jnp.zeros((seq, half), jnp.float32), sin,
                            jnp.zeros((seq, HEAD_DIM - ROT_DIM), jnp.float32)], axis=1)
    return cos_t, sa_t, sb_t


def _moba_kernel(q_ref, k_ref, v_ref, o_ref, kmh_ref, kml_ref):
    qi = pl.program_id(2)
    blk = MOBA_BLOCK
    nb = k_ref.shape[1] // blk
    scale = HEAD_DIM ** -0.5

    @pl.when(qi == 0)
    def _():
        kmh_ref[...] = jnp.zeros_like(kmh_ref)
        kml_ref[...] = jnp.zeros_like(kml_ref)
        for n in range(nb):
            km = jnp.mean(k_ref[0, n * blk:(n + 1) * blk, :].astype(jnp.float32), axis=0, keepdims=True)
            hi = km.astype(jnp.bfloat16)
            kmh_ref[n:n + 1, :] = hi
            kml_ref[n:n + 1, :] = (km - hi.astype(jnp.float32)).astype(jnp.bfloat16)

    q = q_ref[0]
    gate = _dot_nt(q, kmh_ref[...]) + _dot_nt(q, kml_ref[...])
    lane = lax.broadcasted_iota(jnp.int32, gate.shape, 1)
    lane_f = lane.astype(jnp.float32)
    past = lane < qi
    g = jnp.where(past, gate, -jnp.inf)
    sel = jnp.zeros(gate.shape, jnp.float32)
    for _ in range(MOBA_TOPK):
        m = jnp.max(g, axis=1, keepdims=True)
        first = jnp.min(jnp.where(g == m, lane_f, float(LANES)), axis=1, keepdims=True)
        hit = lane_f == first
        sel = jnp.where(hit & past, 1.0, sel)
        g = jnp.where(hit, -jnp.inf, g)

    row = lax.broadcasted_iota(jnp.int32, (blk, blk), 0)
    col = lax.broadcasted_iota(jnp.int32, (blk, blk), 1)
    k_own = k_ref[0, pl.ds(pl.multiple_of(qi * blk, blk), blk), :]
    v_own = v_ref[0, pl.ds(pl.multiple_of(qi * blk, blk), blk), :]
    s = jnp.where(col <= row, _dot_nt(q, k_own) * scale, -jnp.inf)
    m0 = jnp.max(s, axis=1, keepdims=True)
    p = jnp.exp(s - m0)
    l0 = jnp.sum(p, axis=1, keepdims=True)
    acc0 = _dot(p.astype(jnp.bfloat16), v_own)

    def body(kj, carry):
        m_prev, l_prev, acc_prev = carry
        start = pl.multiple_of(kj * blk, blk)
        k_j = k_ref[0, pl.ds(start, blk), :]
        v_j = v_ref[0, pl.ds(start, blk), :]
        chosen = jnp.sum(jnp.where(lane == kj, sel, 0.0), axis=1, keepdims=True) > 0.0
        s = jnp.where(chosen, _dot_nt(q, k_j) * scale, -jnp.inf)
        m_new = jnp.maximum(m_prev, jnp.max(s, axis=1, keepdims=True))
        alpha = jnp.exp(m_prev - m_new)
        p = jnp.exp(s - m_new)
        l_new = alpha * l_prev + jnp.sum(p, axis=1, keepdims=True)
        acc_new = alpha * acc_prev + _dot(p.astype(jnp.bfloat16), v_j)
        return m_new, l_new, acc_new

    _, l_fin, acc = lax.fori_loop(0, qi, body, (m0, l0, acc0))
    o_ref[0] = (acc / l_fin).astype(o_ref.dtype)


def _moba(qkv, n_heads, q_col, k_col, v_col):
    B, S, _ = qkv.shape
    blk = MOBA_BLOCK
    assert S % blk == 0 and S // blk <= LANES
    return pl.pallas_call(
        _moba_kernel,
        grid=(B, n_heads, S // blk),
        in_specs=[
            pl.BlockSpec((1, blk, HEAD_DIM), lambda b, h, i: (b, i, q_col + h)),
            pl.BlockSpec((1, S, HEAD_DIM), lambda b, h, i: (b, 0, k_col + h)),
            pl.BlockSpec((1, S, HEAD_DIM), lambda b, h, i: (b, 0, v_col + h)),
        ],
        out_specs=pl.BlockSpec((1, blk, HEAD_DIM), lambda b, h, i: (b, i, h)),
        out_shape=jax.ShapeDtypeStruct((B, S, n_heads * HEAD_DIM), jnp.bfloat16),
        scratch_shapes=[pltpu.VMEM((LANES, HEAD_DIM), jnp.bfloat16),
                        pltpu.VMEM((LANES, HEAD_DIM), jnp.bfloat16)],
        compiler_params=pltpu.CompilerParams(
            dimension_semantics=("parallel", "parallel", "arbitrary")),
        name="moba",
    )(qkv, qkv, qkv)


def _sb_kernel(q_ref, k_ref, v_ref, tri_ref, o_ref):
    qi = pl.program_id(2)
    blk = SB_BLOCK
    scale = HEAD_DIM ** -0.5
    q = q_ref[0]
    tri = tri_ref[...]

    def block(kj, rest, acc, diagonal):
        start = pl.multiple_of(kj * blk, blk)
        k_j = k_ref[0, pl.ds(start, blk), :]
        v_j = v_ref[0, pl.ds(start, blk), :]
        z = _dot_nt(q, k_j) * scale
        t = jnp.log1p(jnp.exp(-jnp.abs(z)))
        log_beta = jnp.minimum(z, 0.0) - t
        log_rest = -jnp.maximum(z, 0.0) - t
        if diagonal:
            row = lax.broadcasted_iota(jnp.int32, (blk, blk), 0)
            col = lax.broadcasted_iota(jnp.int32, (blk, blk), 1)
            causal = col < row
            log_rest = jnp.where(causal, log_rest, 0.0)
        hi = log_rest.astype(jnp.bfloat16)
        lo = (log_rest - hi.astype(jnp.float32)).astype(jnp.bfloat16)
        stick = _dot(hi, tri) + _dot(lo, tri) + rest
        a = jnp.exp(log_beta + stick)
        if diagonal:
            a = jnp.where(causal, a, 0.0)
        acc = acc + _dot(a.astype(jnp.bfloat16), v_j)
        rest = rest + jnp.sum(log_rest, axis=1, keepdims=True)
        return rest, acc

    rest0 = jnp.zeros((blk, 1), jnp.float32)
    acc0 = jnp.zeros((blk, HEAD_DIM), jnp.float32)
    rest, acc = block(qi, rest0, acc0, True)

    def body(step, carry):
        return block(qi - 1 - step, carry[0], carry[1], False)

    _, acc = lax.fori_loop(0, qi, body, (rest, acc))
    o_ref[0] = acc.astype(o_ref.dtype)


def _sb(qkv, n_heads, q_col, k_col, v_col):
    B, S, _ = qkv.shape
    blk = SB_BLOCK
    assert S % blk == 0
    idx = jnp.arange(blk)
    tri = (idx[:, None] > idx[None, :]).astype(jnp.bfloat16)
    return pl.pallas_call(
        _sb_kernel,
        grid=(B, n_heads, S // blk),
        in_specs=[
            pl.BlockSpec((1, blk, HEAD_DIM), lambda b, h, i: (b, i, q_col + h)),
            pl.BlockSpec((1, S, HEAD_DIM), lambda b, h, i: (b, 0, k_col + h)),
            pl.BlockSpec((1, S, HEAD_DIM), lambda b, h, i: (b, 0, v_col + h)),
            pl.BlockSpec((blk, blk), lambda b, h, i: (0, 0)),
        ],
        out_specs=pl.BlockSpec((1, blk, HEAD_DIM), lambda b, h, i: (b, i, h)),
        out_shape=jax.ShapeDtypeStruct((B, S, n_heads * HEAD_DIM), jnp.bfloat16),
        compiler_params=pltpu.CompilerParams(
            dimension_semantics=("parallel", "parallel", "arbitrary")),
        name="sb",
    )(qkv, qkv, qkv, tri)


def _out_kernel(x_ref, oa_ref, ob_ref, ga_ref, gb_ref, wa_ref, wb_ref, g_ref, o_ref):
    na = _rms(oa_ref[...].astype(jnp.float32), ga_ref[...]).astype(jnp.bfloat16)
    nb = _rms(ob_ref[...].astype(jnp.float32), gb_ref[...]).astype(jnp.bfloat16)
    y = _dot(na, wa_ref[...]) + _dot(nb, wb_ref[...])
    o_ref[...] = x_ref[...] + _rms(y, g_ref[...])


def _out(x, oa, ob, ga, gb, w, g):
    T, D = x.shape
    Wa, Wb = oa.shape[1], ob.shape[1]
    assert Wa == Wb and w.shape[0] == Wa + Wb
    tm = OUT_TM
    assert T % tm == 0
    return pl.pallas_call(
        _out_kernel,
        grid=(T // tm,),
        in_specs=[
            pl.BlockSpec((tm, D), lambda i: (i, 0)),
            pl.BlockSpec((tm, Wa), lambda i: (i, 0)),
            pl.BlockSpec((tm, Wb), lambda i: (i, 0)),
            pl.BlockSpec((1, Wa), lambda i: (0, 0)),
            pl.BlockSpec((1, Wb), lambda i: (0, 0)),
            pl.BlockSpec((Wa, D), lambda i: (0, 0)),
            pl.BlockSpec((Wb, D), lambda i: (1, 0)),
            pl.BlockSpec((1, D), lambda i: (0, 0)),
        ],
        out_specs=pl.BlockSpec((tm, D), lambda i: (i, 0)),
        out_shape=jax.ShapeDtypeStruct((T, D), jnp.float32),
        compiler_params=pltpu.CompilerParams(
            dimension_semantics=("parallel",), vmem_limit_bytes=VMEM_LIMIT),
        name="out_proj",
    )(x, oa, ob, ga, gb, w, w, g)


def kernel(x, ffn1_pre_g, ffn1_w_gate, ffn1_w_up, ffn1_w_down, ffn1_post_g, mix_pre_g, w_in, moba_out_g, sb_out_g, w_out, mix_post_g, ffn2_pre_g, ffn2_w_gate, ffn2_w_up, ffn2_w_down, ffn2_post_g):
    B, S, D = x.shape
    depth = w_in.shape[0]
    w_moba = moba_out_g.shape[1]
    w_sb = sb_out_g.shape[1]
    h_moba, h_sb = w_moba // HEAD_DIM, w_sb // HEAD_DIM
    bf = lambda w: w.astype(jnp.bfloat16)
    cos_t, sa_t, sb_t = _rope_tables(S)

    xt = x.reshape(B * S, D)
    for l in range(depth):
        xt = _ffn(xt, ffn1_pre_g[l:l + 1], bf(ffn1_w_gate[l]), bf(ffn1_w_up[l]), bf(ffn1_w_down[l]),
                  ffn1_post_g[l:l + 1])
        qkv = _proj(xt, mix_pre_g[l:l + 1], bf(w_in[l]), cos_t, sa_t, sb_t, S, 2 * w_moba)
        qkv = qkv.reshape(B, S, -1)
        o_a = _moba(qkv, h_moba, 0, h_moba, 2 * h_moba)
        o_b = _sb(qkv, h_sb, 3 * h_moba, 3 * h_moba + h_sb, 3 * h_moba + 2 * h_sb)
        xt = _out(xt, o_a.reshape(B * S, w_moba), o_b.reshape(B * S, w_sb),
                  moba_out_g[l:l + 1], sb_out_g[l:l + 1], bf(w_out[l]), mix_post_g[l:l + 1])
        xt = _ffn(xt, ffn2_pre_g[l:l + 1], bf(ffn2_w_gate[l]), bf(ffn2_w_up[l]), bf(ffn2_w_down[l]),
                  ffn2_post_g[l:l + 1])
    return xt.reshape(B, S, D)
```

```python
import functools

import jax
import jax.numpy as jnp
from jax import lax
from jax.experimental import pallas as pl
from jax.experimental.pallas import tpu as pltpu

HEAD_DIM = 128
MOBA_BLOCK = 256
MOBA_TOPK = 3
ROPE_THETA = 500000.0
ROT_DIM = HEAD_DIM // 4
RMS_EPS = 1e-6
FFN_RES_SCALE = 0.5

LANES = 128
LOG2_E = 1.4426950408889634
NEG_BIG = -1e30
MOBA_GROUP = 4
SB_BLOCK = 256
SB_GROUP = 4
VMEM_LIMIT = 56 * 1024 * 1024

FFN_TM, FFN_TF = 512, 512
PROJ_TM, PROJ_TN = 1024, 512
OUT_TM = 512


def _rms(xf, g):
    return xf * lax.rsqrt(jnp.mean(xf * xf, axis=-1, keepdims=True) + RMS_EPS) * g


def _dot(a, b):
    return jnp.dot(a, b, preferred_element_type=jnp.float32)


def _dot_nt(a, b):
    return lax.dot_general(a, b, (((1,), (1,)), ((), ())), preferred_element_type=jnp.float32)


def _ffn_kernel(x_ref, pre_g_ref, wg_ref, wu_ref, wd_ref, post_g_ref, o_ref, xn_ref, acc_ref):
    j = pl.program_id(1)

    @pl.when(j == 0)
    def _():
        xn_ref[...] = _rms(x_ref[...], pre_g_ref[...]).astype(xn_ref.dtype)
        acc_ref[...] = jnp.zeros_like(acc_ref)

    xn = xn_ref[...]
    g = _dot(xn, wg_ref[...])
    u = _dot(xn, wu_ref[...])
    h = (g * jax.nn.sigmoid(g) * u).astype(jnp.bfloat16)
    acc_ref[...] += _dot(h, wd_ref[...])

    @pl.when(j == pl.num_programs(1) - 1)
    def _():
        o_ref[...] = x_ref[...] + FFN_RES_SCALE * _rms(acc_ref[...], post_g_ref[...])


def _ffn(x, pre_g, wg, wu, wd, post_g):
    T, D = x.shape
    F = wg.shape[1]
    tm, tf = FFN_TM, FFN_TF
    assert T % tm == 0 and F % tf == 0
    return pl.pallas_call(
        _ffn_kernel,
        grid=(T // tm, F // tf),
        in_specs=[
            pl.BlockSpec((tm, D), lambda i, j: (i, 0)),
            pl.BlockSpec((1, D), lambda i, j: (0, 0)),
            pl.BlockSpec((D, tf), lambda i, j: (0, j)),
            pl.BlockSpec((D, tf), lambda i, j: (0, j)),
            pl.BlockSpec((tf, D), lambda i, j: (j, 0)),
            pl.BlockSpec((1, D), lambda i, j: (0, 0)),
        ],
        out_specs=pl.BlockSpec((tm, D), lambda i, j: (i, 0)),
        out_shape=jax.ShapeDtypeStruct((T, D), jnp.float32),
        scratch_shapes=[pltpu.VMEM((tm, D), jnp.bfloat16), pltpu.VMEM((tm, D), jnp.float32)],
        compiler_params=pltpu.CompilerParams(
            dimension_semantics=("parallel", "arbitrary"), vmem_limit_bytes=VMEM_LIMIT),
        name="ffn",
    )(x, pre_g, wg, wu, wd, post_g)


def _proj_kernel(n_rope_tiles, x_ref, g_ref, w_ref, cos_ref, sa_ref, sb_ref, o_ref, xn_ref):
    j = pl.program_id(1)

    @pl.when(j == 0)
    def _():
        xn_ref[...] = _rms(x_ref[...], g_ref[...]).astype(xn_ref.dtype)

    y = _dot(xn_ref[...], w_ref[...])

    @pl.when(j < n_rope_tiles)
    def _():
        c, sa, sb = cos_ref[...], sa_ref[...], sb_ref[...]
        for h in range(y.shape[1] // HEAD_DIM):
            yh = y[:, h * HEAD_DIM:(h + 1) * HEAD_DIM]
            r = (yh * c + pltpu.roll(yh, HEAD_DIM - ROT_DIM // 2, 1) * sa
                 + pltpu.roll(yh, ROT_DIM // 2, 1) * sb)
            o_ref[:, h * HEAD_DIM:(h + 1) * HEAD_DIM] = r.astype(o_ref.dtype)

    @pl.when(j >= n_rope_tiles)
    def _():
        o_ref[...] = y.astype(o_ref.dtype)


def _proj(x, g, w, cos_t, sa_t, sb_t, seq, n_rope_cols):
    T, D = x.shape
    N = w.shape[1]
    tm, tn = PROJ_TM, PROJ_TN
    assert T % tm == 0 and N % tn == 0 and seq % tm == 0 and n_rope_cols % tn == 0
    pos_tiles = seq // tm
    tab_spec = pl.BlockSpec((tm, HEAD_DIM), lambda i, j: (i % pos_tiles, 0))
    return pl.pallas_call(
        functools.partial(_proj_kernel, n_rope_cols // tn),
        grid=(T // tm, N // tn),
        in_specs=[
            pl.BlockSpec((tm, D), lambda i, j: (i, 0)),
            pl.BlockSpec((1, D), lambda i, j: (0, 0)),
            pl.BlockSpec((D, tn), lambda i, j: (0, j)),
            tab_spec, tab_spec, tab_spec,
        ],
        out_specs=pl.BlockSpec((tm, tn), lambda i, j: (i, j)),
        out_shape=jax.ShapeDtypeStruct((T, N), jnp.bfloat16),
        scratch_shapes=[pltpu.VMEM((tm, D), jnp.bfloat16)],
        compiler_params=pltpu.CompilerParams(
            dimension_semantics=("parallel", "arbitrary"), vmem_limit_bytes=VMEM_LIMIT),
        name="proj",
    )(x, g, w, cos_t, sa_t, sb_t)


def _rope_tables(seq):
    half = ROT_DIM // 2
    inv_freq = ROPE_THETA ** (-jnp.arange(0, ROT_DIM, 2, dtype=jnp.float32) / ROT_DIM)
    ang = jnp.arange(seq, dtype=jnp.float32)[:, None] * inv_freq[None, :]
    cos, sin = jnp.cos(ang), jnp.sin(ang)
    zeros = jnp.zeros((seq, HEAD_DIM - half), jnp.float32)
    cos_t = jnp.concatenate([cos, cos, jnp.ones((seq, HEAD_DIM - ROT_DIM), jnp.float32)], axis=1)
    sa_t = jnp.concatenate([-sin, zeros], axis=1)
    sb_t = jnp.concatenate([jnp.zeros((seq, half), jnp.float32), sin,
                            jnp.zeros((seq, HEAD_DIM - ROT_DIM), jnp.float32)], axis=1)
    return cos_t, sa_t, sb_t


def _moba_kernel(n_group, q_ref, k_ref, v_ref, o_ref, kaug_ref, kmh_ref, kml_ref):
    qi = pl.program_id(2)
    blk = MOBA_BLOCK
    S = k_ref.shape[1]
    nb = S // blk
    exp2_scale = HEAD_DIM ** -0.5 * LOG2_E
    heads = [slice(g * HEAD_DIM, (g + 1) * HEAD_DIM) for g in range(n_group)]

    @pl.when(qi == 0)
    def _():
        key_block = lax.shift_right_logical(lax.broadcasted_iota(jnp.int32, (S, LANES), 0),
                                            blk.bit_length() - 1)
        one_hot = (key_block == lax.broadcasted_iota(jnp.int32, (S, LANES), 1)).astype(jnp.bfloat16)
        kmh_ref[...] = jnp.zeros_like(kmh_ref)
        kml_ref[...] = jnp.zeros_like(kml_ref)
        for g, hs in enumerate(heads):
            kaug_ref[g, :, :HEAD_DIM] = k_ref[0, :, hs]
            kaug_ref[g, :, HEAD_DIM:] = one_hot
            for n in range(nb):
                km = jnp.mean(k_ref[0, n * blk:(n + 1) * blk, hs].astype(jnp.float32), axis=0, keepdims=True)
                hi = km.astype(jnp.bfloat16)
                kmh_ref[g, n:n + 1, :] = hi
                kml_ref[g, n:n + 1, :] = (km - hi.astype(jnp.float32)).astype(jnp.bfloat16)

    lane = lax.broadcasted_iota(jnp.int32, (blk, LANES), 1)
    lane_f = lane.astype(jnp.float32)
    past = lane < qi
    q_augs = []
    for g, hs in enumerate(heads):
        q = q_ref[0, :, hs]
        gate = _dot_nt(q, kmh_ref[g]) + _dot_nt(q, kml_ref[g])
        gate = jnp.where(past, gate, -jnp.inf)
        attend = lane == qi
        for _ in range(MOBA_TOPK):
            m = jnp.max(gate, axis=1, keepdims=True)
            first = jnp.min(jnp.where(gate == m, lane_f, float(LANES)), axis=1, keepdims=True)
            hit = lane_f == first
            attend = attend | (hit & past)
            gate = jnp.where(hit, -jnp.inf, gate)
        penalty = jnp.where(attend, 0.0, NEG_BIG).astype(jnp.bfloat16)
        q_augs.append(jnp.concatenate([q, penalty], axis=1))

    def scores(kj):
        start = pl.multiple_of(kj * blk, blk)
        return start, [_dot_nt(q_augs[g], kaug_ref[g, pl.ds(start, blk), :]) for g in range(n_group)]

    row = lax.broadcasted_iota(jnp.int32, (blk, blk), 0)
    col = lax.broadcasted_iota(jnp.int32, (blk, blk), 1)
    start, ss = scores(qi)
    carry = []
    for g, hs in enumerate(heads):
        s = jnp.where(col <= row, ss[g], NEG_BIG)
        m0 = jnp.max(s, axis=1, keepdims=True)
        p = jnp.exp2((s - m0) * exp2_scale)
        l0 = jnp.sum(p, axis=1, keepdims=True)
        carry.append((m0, l0, _dot(p.astype(jnp.bfloat16), v_ref[0, pl.ds(start, blk), hs])))

    def body(kj, carry):
        start, ss = scores(kj)
        out = []
        for g, hs in enumerate(heads):
            m_prev, l_prev, acc_prev = carry[g]
            m_new = jnp.maximum(m_prev, jnp.max(ss[g], axis=1, keepdims=True))
            alpha = jnp.exp2((m_prev - m_new) * exp2_scale)
            p = jnp.exp2((ss[g] - m_new) * exp2_scale)
            l_new = alpha * l_prev + jnp.sum(p, axis=1, keepdims=True)
            acc_new = alpha * acc_prev + _dot(p.astype(jnp.bfloat16), v_ref[0, pl.ds(start, blk), hs])
            out.append((m_new, l_new, acc_new))
        return tuple(out)

    carry = lax.fori_loop(0, qi, body, tuple(carry))
    for g, hs in enumerate(heads):
        _, l_fin, acc = carry[g]
        o_ref[0, :, hs] = (acc / l_fin).astype(o_ref.dtype)


def _moba(qkv, n_heads, q_col, k_col, v_col):
    B, S, _ = qkv.shape
    blk, G = MOBA_BLOCK, MOBA_GROUP
    assert S % blk == 0 and S // blk <= LANES and blk & (blk - 1) == 0
    assert n_heads % G == 0 and q_col % G == 0 and k_col % G == 0 and v_col % G == 0
    W = G * HEAD_DIM
    return pl.pallas_call(
        functools.partial(_moba_kernel, G),
        grid=(B, n_heads // G, S // blk),
        in_specs=[
            pl.BlockSpec((1, blk, W), lambda b, h, i: (b, i, q_col // G + h)),
            pl.BlockSpec((1, S, W), lambda b, h, i: (b, 0, k_col // G + h)),
            pl.BlockSpec((1, S, W), lambda b, h, i: (b, 0, v_col // G + h)),
        ],
        out_specs=pl.BlockSpec((1, blk, W), lambda b, h, i: (b, i, h)),
        out_shape=jax.ShapeDtypeStruct((B, S, n_heads * HEAD_DIM), jnp.bfloat16),
        scratch_shapes=[pltpu.VMEM((G, S, HEAD_DIM + LANES), jnp.bfloat16),
                        pltpu.VMEM((G, LANES, HEAD_DIM), jnp.bfloat16),
                        pltpu.VMEM((G, LANES, HEAD_DIM), jnp.bfloat16)],
        compiler_params=pltpu.CompilerParams(
            dimension_semantics=("parallel", "parallel", "arbitrary")),
        name="moba",
    )(qkv, qkv, qkv)


def _sb_kernel(n_group, q_ref, k_ref, v_ref, tri_ref, o_ref):
    qi = pl.program_id(2)
    blk = SB_BLOCK
    scale = HEAD_DIM ** -0.5
    neg_tri = tri_ref[...]
    heads = [slice(g * HEAD_DIM, (g + 1) * HEAD_DIM) for g in range(n_group)]
    qs = [q_ref[0, :, hs] for hs in heads]

    def block(kj, carry, diagonal):
        start = pl.multiple_of(kj * blk, blk)
        if diagonal:
            row = lax.broadcasted_iota(jnp.int32, (blk, blk), 0)
            col = lax.broadcasted_iota(jnp.int32, (blk, blk), 1)
            causal = col < row
        zs = [_dot_nt(qs[g], k_ref[0, pl.ds(start, blk), hs]) * scale for g, hs in enumerate(heads)]
        log_betas, row_sums, sticks = [], [], []
        for g in range(n_group):
            z = zs[g]
            softplus = jnp.maximum(z, 0.0) + jnp.log(1.0 + jnp.exp2(jnp.abs(z) * -LOG2_E))
            log_betas.append(z - softplus)
            if diagonal:
                softplus = jnp.where(causal, softplus, 0.0)
            row_sums.append(jnp.sum(softplus, axis=1, keepdims=True))
            sticks.append(_dot(softplus.astype(jnp.bfloat16), neg_tri))
        out = []
        for g, hs in enumerate(heads):
            rest, acc = carry[g]
            a = jnp.exp(log_betas[g] + sticks[g] + rest)
            if diagonal:
                a = jnp.where(causal, a, 0.0)
            acc = acc + _dot(a.astype(jnp.bfloat16), v_ref[0, pl.ds(start, blk), hs])
            out.append((rest - row_sums[g], acc))
        return tuple(out)

    init = tuple((jnp.zeros((blk, 1), jnp.float32), jnp.zeros((blk, HEAD_DIM), jnp.float32))
                 for _ in heads)
    carry = block(qi, init, True)
    carry = lax.fori_loop(0, qi, lambda step, c: block(qi - 1 - step, c, False), carry)
    for g, hs in enumerate(heads):
        o_ref[0, :, hs] = carry[g][1].astype(o_ref.dtype)


def _sb(qkv, n_heads, q_col, k_col, v_col):
    B, S, _ = qkv.shape
    blk, G = SB_BLOCK, SB_GROUP
    assert S % blk == 0 and n_heads % G == 0 and q_col % G == 0 and k_col % G == 0 and v_col % G == 0
    idx = jnp.arange(blk)
    neg_tri = -(idx[:, None] > idx[None, :]).astype(jnp.bfloat16)
    W = G * HEAD_DIM
    return pl.pallas_call(
        functools.partial(_sb_kernel, G),
        grid=(B, n_heads // G, S // blk),
        in_specs=[
            pl.BlockSpec((1, blk, W), lambda b, h, i: (b, i, q_col // G + h)),
            pl.BlockSpec((1, S, W), lambda b, h, i: (b, 0, k_col // G + h)),
            pl.BlockSpec((1, S, W), lambda b, h, i: (b, 0, v_col // G + h)),
            pl.BlockSpec((blk, blk), lambda b, h, i: (0, 0)),
        ],
        out_specs=pl.BlockSpec((1, blk, W), lambda b, h, i: (b, i, h)),
        out_shape=jax.ShapeDtypeStruct((B, S, n_heads * HEAD_DIM), jnp.bfloat16),
        compiler_params=pltpu.CompilerParams(
            dimension_semantics=("parallel", "parallel", "arbitrary")),
        name="sb",
    )(qkv, qkv, qkv, neg_tri)


def _out_kernel(x_ref, oa_ref, ob_ref, ga_ref, gb_ref, wa_ref, wb_ref, g_ref, o_ref):
    na = _rms(oa_ref[...].astype(jnp.float32), ga_ref[...]).astype(jnp.bfloat16)
    nb = _rms(ob_ref[...].astype(jnp.float32), gb_ref[...]).astype(jnp.bfloat16)
    y = _dot(na, wa_ref[...]) + _dot(nb, wb_ref[...])
    o_ref[...] = x_ref[...] + _rms(y, g_ref[...])


def _out(x, oa, ob, ga, gb, w, g):
    T, D = x.shape
    Wa, Wb = oa.shape[1], ob.shape[1]
    assert Wa == Wb and w.shape[0] == Wa + Wb
    tm = OUT_TM
    assert T % tm == 0
    return pl.pallas_call(
        _out_kernel,
        grid=(T // tm,),
        in_specs=[
            pl.BlockSpec((tm, D), lambda i: (i, 0)),
            pl.BlockSpec((tm, Wa), lambda i: (i, 0)),
            pl.BlockSpec((tm, Wb), lambda i: (i, 0)),
            pl.BlockSpec((1, Wa), lambda i: (0, 0)),
            pl.BlockSpec((1, Wb), lambda i: (0, 0)),
            pl.BlockSpec((Wa, D), lambda i: (0, 0)),
            pl.BlockSpec((Wb, D), lambda i: (1, 0)),
            pl.BlockSpec((1, D), lambda i: (0, 0)),
        ],
        out_specs=pl.BlockSpec((tm, D), lambda i: (i, 0)),
        out_shape=jax.ShapeDtypeStruct((T, D), jnp.float32),
        compiler_params=pltpu.CompilerParams(
            dimension_semantics=("parallel",), vmem_limit_bytes=VMEM_LIMIT),
        name="out_proj",
    )(x, oa, ob, ga, gb, w, w, g)


def kernel(x, ffn1_pre_g, ffn1_w_gate, ffn1_w_up, ffn1_w_down, ffn1_post_g, mix_pre_g, w_in, moba_out_g, sb_out_g, w_out, mix_post_g, ffn2_pre_g, ffn2_w_gate, ffn2_w_up, ffn2_w_down, ffn2_post_g):
    B, S, D = x.shape
    depth = w_in.shape[0]
    w_moba = moba_out_g.shape[1]
    w_sb = sb_out_g.shape[1]
    h_moba, h_sb = w_moba // HEAD_DIM, w_sb // HEAD_DIM
    bf = lambda w: w.astype(jnp.bfloat16)
    cos_t, sa_t, sb_t = _rope_tables(S)

    xt = x.reshape(B * S, D)
    for l in range(depth):
        xt = _ffn(xt, ffn1_pre_g[l:l + 1], bf(ffn1_w_gate[l]), bf(ffn1_w_up[l]), bf(ffn1_w_down[l]),
                  ffn1_post_g[l:l + 1])
        qkv = _proj(xt, mix_pre_g[l:l + 1], bf(w_in[l]), cos_t, sa_t, sb_t, S, 2 * w_moba)
        qkv = qkv.reshape(B, S, -1)
        o_a = _moba(qkv, h_moba, 0, h_moba, 2 * h_moba)
        o_b = _sb(qkv, h_sb, 3 * h_moba, 3 * h_moba + h_sb, 3 * h_moba + 2 * h_sb)
        xt = _out(xt, o_a.reshape(B * S, w_moba), o_b.reshape(B * S, w_sb),
                  moba_out_g[l:l + 1], sb_out_g[l:l + 1], bf(w_out[l]), mix_post_g[l:l + 1])
        xt = _ffn(xt, ffn2_pre_g[l:l + 1], bf(ffn2_w_gate[l]), bf(ffn2_w_up[l]), bf(ffn2_w_down[l]),
                  ffn2_post_g[l:l + 1])
    return xt.reshape(B, S, D)
```

```python
import functools

import jax
import jax.numpy as jnp
from jax import lax
from jax.experimental import pallas as pl
from jax.experimental.pallas import tpu as pltpu

HEAD_DIM = 128
MOBA_BLOCK = 256
MOBA_TOPK = 3
ROPE_THETA = 500000.0
ROT_DIM = HEAD_DIM // 4
RMS_EPS = 1e-6
FFN_RES_SCALE = 0.5

LANES = 128
LOG2_E = 1.4426950408889634
NEG_BIG = -1e30
MOBA_GROUP = 4
SB_BLOCK = 256
SB_GROUP = 4
VMEM_LIMIT = 56 * 1024 * 1024

FFN_TM, FFN_TF, FFN_TN = 512, 512, 512
PROJ_TM, PROJ_TN = 1024, 1024
MXU_COLS = 256
OUT_TM = 512


def _rms(xf, g):
    return xf * lax.rsqrt(jnp.mean(xf * xf, axis=-1, keepdims=True) + RMS_EPS) * g


def _dot(a, b):
    return jnp.dot(a, b, preferred_element_type=jnp.float32)


def _dot_nt(a, b):
    return lax.dot_general(a, b, (((1,), (1,)), ((), ())), preferred_element_type=jnp.float32)


def _ffn_kernel(n_up, prenorm, emit_next, *refs):
    x_ref, a_ref, wg_ref, wu_ref, wd_ref, post_g_ref = refs[:6]
    refs = refs[6:]
    next_g_ref = None
    if emit_next:
        next_g_ref, refs = refs[0], refs[1:]
    o_ref, refs = refs[0], refs[1:]
    on_ref = None
    if emit_next:
        on_ref, refs = refs[0], refs[1:]
    if prenorm:
        xn_ref, refs = refs[0], refs[1:]
    else:
        xn_ref = a_ref
    h_ref, f_ref = refs
    j = pl.program_id(1)

    if prenorm:
        @pl.when(j == 0)
        def _():
            xn_ref[...] = _rms(x_ref[...], a_ref[...]).astype(xn_ref.dtype)

    @pl.when(j < n_up)
    def _():
        xn = xn_ref[...]
        g = _dot(xn, wg_ref[...])
        u = _dot(xn, wu_ref[...])
        h_ref[j] = (g * jax.nn.sigmoid(g) * u).astype(h_ref.dtype)

    @pl.when(j >= n_up)
    def _():
        h = jnp.concatenate([h_ref[k] for k in range(n_up)], axis=1)
        f_ref[j - n_up] = _dot(h, wd_ref[...])

    @pl.when(j == pl.num_programs(1) - 1)
    def _():
        f = jnp.concatenate([f_ref[n] for n in range(f_ref.shape[0])], axis=1)
        y = x_ref[...] + FFN_RES_SCALE * _rms(f, post_g_ref[...])
        o_ref[...] = y
        if emit_next:
            on_ref[...] = _rms(y, next_g_ref[...]).astype(on_ref.dtype)


def _ffn(x, xn, pre_g, wg, wu, wd, post_g, next_g):
    T, D = x.shape
    F = wg.shape[1]
    tm, tf, tn = FFN_TM, FFN_TF, FFN_TN
    assert T % tm == 0 and F % tf == 0 and D % tn == 0
    assert (xn is None) != (pre_g is None)
    prenorm, emit_next = xn is None, next_g is not None
    n_up, n_down = F // tf, D // tn
    row_spec = pl.BlockSpec((tm, D), lambda i, j: (i, 0))
    gain_spec = pl.BlockSpec((1, D), lambda i, j: (0, 0))
    up_spec = pl.BlockSpec((D, tf), lambda i, j: (0, jnp.minimum(j, n_up - 1)))
    in_specs = [row_spec, gain_spec if prenorm else row_spec, up_spec, up_spec,
                pl.BlockSpec((F, tn), lambda i, j: (0, jnp.maximum(j - n_up, 0))), gain_spec]
    args = [x, pre_g if prenorm else xn, wg, wu, wd, post_g]
    out_specs, out_shape = [row_spec], [jax.ShapeDtypeStruct((T, D), jnp.float32)]
    scratch = [pltpu.VMEM((n_up, tm, tf), jnp.bfloat16), pltpu.VMEM((n_down, tm, tn), jnp.float32)]
    if emit_next:
        in_specs.append(gain_spec)
        args.append(next_g)
        out_specs.append(row_spec)
        out_shape.append(jax.ShapeDtypeStruct((T, D), jnp.bfloat16))
    if prenorm:
        scratch.insert(0, pltpu.VMEM((tm, D), jnp.bfloat16))
    outs = pl.pallas_call(
        functools.partial(_ffn_kernel, n_up, prenorm, emit_next),
        grid=(T // tm, n_up + n_down),
        in_specs=in_specs,
        out_specs=out_specs,
        out_shape=out_shape,
        scratch_shapes=scratch,
        compiler_params=pltpu.CompilerParams(
            dimension_semantics=("parallel", "arbitrary"), vmem_limit_bytes=VMEM_LIMIT),
        name="ffn",
    )(*args)
    return (outs[0], outs[1]) if emit_next else (outs[0], None)


def _proj_kernel(n_rope_tiles, xn_ref, w_ref, cos_ref, sa_ref, sb_ref, o_ref):
    j = pl.program_id(1)

    def tile(rope):
        xn = xn_ref[...]
        ys = [_dot(xn, w_ref[:, c:c + MXU_COLS]) for c in range(0, w_ref.shape[1], MXU_COLS)]
        for n, y in enumerate(ys):
            for h in range(0, MXU_COLS, HEAD_DIM):
                yh = y[:, h:h + HEAD_DIM]
                if rope:
                    yh = (yh * cos_ref[...] + pltpu.roll(yh, HEAD_DIM - ROT_DIM // 2, 1) * sa_ref[...]
                          + pltpu.roll(yh, ROT_DIM // 2, 1) * sb_ref[...])
                o_ref[:, n * MXU_COLS + h:n * MXU_COLS + h + HEAD_DIM] = yh.astype(o_ref.dtype)

    pl.when(j < n_rope_tiles)(functools.partial(tile, True))
    pl.when(j >= n_rope_tiles)(functools.partial(tile, False))


def _proj(xn, w, tables, seq, n_rope_cols):
    T, D = xn.shape
    N = w.shape[1]
    tm, tn = PROJ_TM, PROJ_TN
    assert T % tm == 0 and N % tn == 0 and seq % tm == 0 and n_rope_cols % tn == 0 and tn % MXU_COLS == 0
    pos_tiles = seq // tm
    tab_spec = pl.BlockSpec((tm, HEAD_DIM), lambda i, j: (i % pos_tiles, 0))
    return pl.pallas_call(
        functools.partial(_proj_kernel, n_rope_cols // tn),
        grid=(T // tm, N // tn),
        in_specs=[
            pl.BlockSpec((tm, D), lambda i, j: (i, 0)),
            pl.BlockSpec((D, tn), lambda i, j: (0, j)),
            tab_spec, tab_spec, tab_spec,
        ],
        out_specs=pl.BlockSpec((tm, tn), lambda i, j: (i, j)),
        out_shape=jax.ShapeDtypeStruct((T, N), jnp.bfloat16),
        compiler_params=pltpu.CompilerParams(
            dimension_semantics=("parallel", "arbitrary"), vmem_limit_bytes=VMEM_LIMIT),
        name="proj",
    )(xn, w, *tables)


def _rope_tables(seq):
    half = ROT_DIM // 2
    inv_freq = ROPE_THETA ** (-jnp.arange(0, ROT_DIM, 2, dtype=jnp.float32) / ROT_DIM)
    ang = jnp.arange(seq, dtype=jnp.float32)[:, None] * inv_freq[None, :]
    cos, sin = jnp.cos(ang), jnp.sin(ang)
    zeros = jnp.zeros((seq, HEAD_DIM - half), jnp.float32)
    cos_t = jnp.concatenate([cos, cos, jnp.ones((seq, HEAD_DIM - ROT_DIM), jnp.float32)], axis=1)
    sa_t = jnp.concatenate([-sin, zeros], axis=1)
    sb_t = jnp.concatenate([jnp.zeros((seq, half), jnp.float32), sin,
                            jnp.zeros((seq, HEAD_DIM - ROT_DIM), jnp.float32)], axis=1)
    return cos_t, sa_t, sb_t


def _moba_kernel(n_group, q_ref, k_ref, v_ref, o_ref, kaug_ref, kmh_ref, kml_ref):
    qi = pl.program_id(2)
    blk = MOBA_BLOCK
    S = k_ref.shape[1]
    nb = S // blk
    exp2_scale = HEAD_DIM ** -0.5 * LOG2_E
    heads = [slice(g * HEAD_DIM, (g + 1) * HEAD_DIM) for g in range(n_group)]

    @pl.when(qi == 0)
    def _():
        key_block = lax.shift_right_logical(lax.broadcasted_iota(jnp.int32, (S, LANES), 0),
                                            blk.bit_length() - 1)
        one_hot = (key_block == lax.broadcasted_iota(jnp.int32, (S, LANES), 1)).astype(jnp.bfloat16)
        kmh_ref[...] = jnp.zeros_like(kmh_ref)
        kml_ref[...] = jnp.zeros_like(kml_ref)
        for g, hs in enumerate(heads):
            kaug_ref[g, :, :HEAD_DIM] = k_ref[0, :, hs]
            kaug_ref[g, :, HEAD_DIM:] = one_hot
            for n in range(nb):
                km = jnp.mean(k_ref[0, n * blk:(n + 1) * blk, hs].astype(jnp.float32), axis=0, keepdims=True)
                hi = km.astype(jnp.bfloat16)
                kmh_ref[g, n:n + 1, :] = hi
                kml_ref[g, n:n + 1, :] = (km - hi.astype(jnp.float32)).astype(jnp.bfloat16)

    lane = lax.broadcasted_iota(jnp.int32, (blk, LANES), 1)
    lane_f = lane.astype(jnp.float32)
    past = lane < qi
    q_augs = []
    for g, hs in enumerate(heads):
        q = q_ref[0, :, hs]
        gate = _dot_nt(q, kmh_ref[g]) + _dot_nt(q, kml_ref[g])
        gate = jnp.where(past, gate, -jnp.inf)
        attend = lane == qi
        for _ in range(MOBA_TOPK):
            m = jnp.max(gate, axis=1, keepdims=True)
            first = jnp.min(jnp.where(gate == m, lane_f, float(LANES)), axis=1, keepdims=True)
            hit = lane_f == first
            attend = attend | (hit & past)
            gate = jnp.where(hit, -jnp.inf, gate)
        penalty = jnp.where(attend, 0.0, NEG_BIG).astype(jnp.bfloat16)
        q_augs.append(jnp.concatenate([q, penalty], axis=1))

    def scores(kj):
        start = pl.multiple_of(kj * blk, blk)
        return start, [_dot_nt(q_augs[g], kaug_ref[g, pl.ds(start, blk), :]) for g in range(n_group)]

    row = lax.broadcasted_iota(jnp.int32, (blk, blk), 0)
    col = lax.broadcasted_iota(jnp.int32, (blk, blk), 1)
    start, ss = scores(qi)
    carry = []
    for g, hs in enumerate(heads):
        s = jnp.where(col <= row, ss[g], NEG_BIG)
        m0 = jnp.max(s, axis=1, keepdims=True)
        p = jnp.exp2((s - m0) * exp2_scale)
        l0 = jnp.sum(p, axis=1, keepdims=True)
        carry.append((m0, l0, _dot(p.astype(jnp.bfloat16), v_ref[0, pl.ds(start, blk), hs])))

    def body(kj, carry):
        start, ss = scores(kj)
        out = []
        for g, hs in enumerate(heads):
            m_prev, l_prev, acc_prev = carry[g]
            m_new = jnp.maximum(m_prev, jnp.max(ss[g], axis=1, keepdims=True))
            alpha = jnp.exp2((m_prev - m_new) * exp2_scale)
            p = jnp.exp2((ss[g] - m_new) * exp2_scale)
            l_new = alpha * l_prev + jnp.sum(p, axis=1, keepdims=True)
            acc_new = alpha * acc_prev + _dot(p.astype(jnp.bfloat16), v_ref[0, pl.ds(start, blk), hs])
            out.append((m_new, l_new, acc_new))
        return tuple(out)

    carry = lax.fori_loop(0, qi, body, tuple(carry))
    for g, hs in enumerate(heads):
        _, l_fin, acc = carry[g]
        o_ref[0, :, hs] = (acc / l_fin).astype(o_ref.dtype)


def _moba(qkv, n_heads, q_col, k_col, v_col):
    B, S, _ = qkv.shape
    blk, G = MOBA_BLOCK, MOBA_GROUP
    assert S % blk == 0 and S // blk <= LANES and blk & (blk - 1) == 0
    assert n_heads % G == 0 and q_col % G == 0 and k_col % G == 0 and v_col % G == 0
    W = G * HEAD_DIM
    return pl.pallas_call(
        functools.partial(_moba_kernel, G),
        grid=(B, n_heads // G, S // blk),
        in_specs=[
            pl.BlockSpec((1, blk, W), lambda b, h, i: (b, i, q_col // G + h)),
            pl.BlockSpec((1, S, W), lambda b, h, i: (b, 0, k_col // G + h)),
            pl.BlockSpec((1, S, W), lambda b, h, i: (b, 0, v_col // G + h)),
        ],
        out_specs=pl.BlockSpec((1, blk, W), lambda b, h, i: (b, i, h)),
        out_shape=jax.ShapeDtypeStruct((B, S, n_heads * HEAD_DIM), jnp.bfloat16),
        scratch_shapes=[pltpu.VMEM((G, S, HEAD_DIM + LANES), jnp.bfloat16),
                        pltpu.VMEM((G, LANES, HEAD_DIM), jnp.bfloat16),
                        pltpu.VMEM((G, LANES, HEAD_DIM), jnp.bfloat16)],
        compiler_params=pltpu.CompilerParams(
            dimension_semantics=("parallel", "parallel", "arbitrary")),
        name="moba",
    )(qkv, qkv, qkv)


def _sb_kernel(n_group, q_ref, k_ref, v_ref, tri_ref, o_ref):
    qi = pl.program_id(2)
    blk = SB_BLOCK
    scale = HEAD_DIM ** -0.5
    neg_tri = tri_ref[...]
    heads = [slice(g * HEAD_DIM, (g + 1) * HEAD_DIM) for g in range(n_group)]
    qs = [q_ref[0, :, hs] for hs in heads]

    def block(kj, carry, diagonal):
        start = pl.multiple_of(kj * blk, blk)
        if diagonal:
            row = lax.broadcasted_iota(jnp.int32, (blk, blk), 0)
            col = lax.broadcasted_iota(jnp.int32, (blk, blk), 1)
            causal = col < row
        zs = [_dot_nt(qs[g], k_ref[0, pl.ds(start, blk), hs]) * scale for g, hs in enumerate(heads)]
        log_betas, row_sums, sticks = [], [], []
        for g in range(n_group):
            z = zs[g]
            softplus = jnp.maximum(z, 0.0) + jnp.log(1.0 + jnp.exp2(jnp.abs(z) * -LOG2_E))
            log_betas.append(z - softplus)
            if diagonal:
                softplus = jnp.where(causal, softplus, 0.0)
            row_sums.append(jnp.sum(softplus, axis=1, keepdims=True))
            sticks.append(_dot(softplus.astype(jnp.bfloat16), neg_tri))
        out = []
        for g, hs in enumerate(heads):
            rest, acc = carry[g]
            a = jnp.exp(log_betas[g] + sticks[g] + rest)
            if diagonal:
                a = jnp.where(causal, a, 0.0)
            acc = acc + _dot(a.astype(jnp.bfloat16), v_ref[0, pl.ds(start, blk), hs])
            out.append((rest - row_sums[g], acc))
        return tuple(out)

    init = tuple((jnp.zeros((blk, 1), jnp.float32), jnp.zeros((blk, HEAD_DIM), jnp.float32))
                 for _ in heads)
    carry = block(qi, init, True)
    carry = lax.fori_loop(0, qi, lambda step, c: block(qi - 1 - step, c, False), carry)
    for g, hs in enumerate(heads):
        o_ref[0, :, hs] = carry[g][1].astype(o_ref.dtype)


def _sb(qkv, n_heads, q_col, k_col, v_col):
    B, S, _ = qkv.shape
    blk, G = SB_BLOCK, SB_GROUP
    assert S % blk == 0 and n_heads % G == 0 and q_col % G == 0 and k_col % G == 0 and v_col % G == 0
    idx = jnp.arange(blk)
    neg_tri = -(idx[:, None] > idx[None, :]).astype(jnp.bfloat16)
    W = G * HEAD_DIM
    return pl.pallas_call(
        functools.partial(_sb_kernel, G),
        grid=(B, n_heads // G, S // blk),
        in_specs=[
            pl.BlockSpec((1, blk, W), lambda b, h, i: (b, i, q_col // G + h)),
            pl.BlockSpec((1, S, W), lambda b, h, i: (b, 0, k_col // G + h)),
            pl.BlockSpec((1, S, W), lambda b, h, i: (b, 0, v_col // G + h)),
            pl.BlockSpec((blk, blk), lambda b, h, i: (0, 0)),
        ],
        out_specs=pl.BlockSpec((1, blk, W), lambda b, h, i: (b, i, h)),
        out_shape=jax.ShapeDtypeStruct((B, S, n_heads * HEAD_DIM), jnp.bfloat16),
        compiler_params=pltpu.CompilerParams(
            dimension_semantics=("parallel", "parallel", "arbitrary")),
        name="sb",
    )(qkv, qkv, qkv, neg_tri)


def _out_kernel(x_ref, oa_ref, ob_ref, ga_ref, gb_ref, wa_ref, wb_ref, g_ref, next_g_ref, o_ref, on_ref):
    na = _rms(oa_ref[...].astype(jnp.float32), ga_ref[...]).astype(jnp.bfloat16)
    nb = _rms(ob_ref[...].astype(jnp.float32), gb_ref[...]).astype(jnp.bfloat16)
    y = x_ref[...] + _rms(_dot(na, wa_ref[...]) + _dot(nb, wb_ref[...]), g_ref[...])
    o_ref[...] = y
    on_ref[...] = _rms(y, next_g_ref[...]).astype(on_ref.dtype)


def _out(x, oa, ob, ga, gb, w, g, next_g):
    T, D = x.shape
    Wa, Wb = oa.shape[1], ob.shape[1]
    assert Wa == Wb and w.shape[0] == Wa + Wb
    tm = OUT_TM
    assert T % tm == 0
    return pl.pallas_call(
        _out_kernel,
        grid=(T // tm,),
        in_specs=[
            pl.BlockSpec((tm, D), lambda i: (i, 0)),
            pl.BlockSpec((tm, Wa), lambda i: (i, 0)),
            pl.BlockSpec((tm, Wb), lambda i: (i, 0)),
            pl.BlockSpec((1, Wa), lambda i: (0, 0)),
            pl.BlockSpec((1, Wb), lambda i: (0, 0)),
            pl.BlockSpec((Wa, D), lambda i: (0, 0)),
            pl.BlockSpec((Wb, D), lambda i: (1, 0)),
            pl.BlockSpec((1, D), lambda i: (0, 0)),
            pl.BlockSpec((1, D), lambda i: (0, 0)),
        ],
        out_specs=[pl.BlockSpec((tm, D), lambda i: (i, 0)), pl.BlockSpec((tm, D), lambda i: (i, 0))],
        out_shape=[jax.ShapeDtypeStruct((T, D), jnp.float32), jax.ShapeDtypeStruct((T, D), jnp.bfloat16)],
        compiler_params=pltpu.CompilerParams(
            dimension_semantics=("parallel",), vmem_limit_bytes=VMEM_LIMIT),
        name="out_proj",
    )(x, oa, ob, ga, gb, w, w, g, next_g)


def kernel(x, ffn1_pre_g, ffn1_w_gate, ffn1_w_up, ffn1_w_down, ffn1_post_g, mix_pre_g, w_in, moba_out_g, sb_out_g, w_out, mix_post_g, ffn2_pre_g, ffn2_w_gate, ffn2_w_up, ffn2_w_down, ffn2_post_g):
    B, S, D = x.shape
    depth = w_in.shape[0]
    w_moba = moba_out_g.shape[1]
    w_sb = sb_out_g.shape[1]
    h_moba, h_sb = w_moba // HEAD_DIM, w_sb // HEAD_DIM
    bf = lambda w: w.astype(jnp.bfloat16)
    tables = _rope_tables(S)

    xt = x.reshape(B * S, D)
    for l in range(depth):
        xt, xn = _ffn(xt, None, ffn1_pre_g[l:l + 1], bf(ffn1_w_gate[l]), bf(ffn1_w_up[l]),
                      bf(ffn1_w_down[l]), ffn1_post_g[l:l + 1], mix_pre_g[l:l + 1])
        qkv = _proj(xn, bf(w_in[l]), tables, S, 2 * w_moba).reshape(B, S, -1)
        o_a = _moba(qkv, h_moba, 0, h_moba, 2 * h_moba)
        o_b = _sb(qkv, h_sb, 3 * h_moba, 3 * h_moba + h_sb, 3 * h_moba + 2 * h_sb)
        xt, xn = _out(xt, o_a.reshape(B * S, w_moba), o_b.reshape(B * S, w_sb),
                      moba_out_g[l:l + 1], sb_out_g[l:l + 1], bf(w_out[l]), mix_post_g[l:l + 1],
                      ffn2_pre_g[l:l + 1])
        xt, _ = _ffn(xt, xn, None, bf(ffn2_w_gate[l]), bf(ffn2_w_up[l]), bf(ffn2_w_down[l]),
                     ffn2_post_g[l:l + 1], None)
    return xt.reshape(B, S, D)
```

```python
import functools

import jax
import jax.numpy as jnp
from jax import lax
from jax.experimental import pallas as pl
from jax.experimental.pallas import tpu as pltpu

HEAD_DIM = 128
MOBA_BLOCK = 256
MOBA_TOPK = 3
ROPE_THETA = 500000.0
ROT_DIM = HEAD_DIM // 4
RMS_EPS = 1e-6
FFN_RES_SCALE = 0.5

LANES = 128
LOG2_E = 1.4426950408889634
NEG_BIG = -1e30
MOBA_GROUP = 4
SB_BLOCK = 256
SB_GROUP = 4
VMEM_LIMIT = 56 * 1024 * 1024

FFN_TM, FFN_TF, FFN_TN = 512, 512, 512
PROJ_TM, PROJ_TN = 1024, 1024
MXU_COLS = 256
OUT_TM = 512


def _rms(xf, g):
    return xf * lax.rsqrt(jnp.mean(xf * xf, axis=-1, keepdims=True) + RMS_EPS) * g


def _dot(a, b):
    return jnp.dot(a, b, preferred_element_type=jnp.float32)


def _dot_nt(a, b):
    return lax.dot_general(a, b, (((1,), (1,)), ((), ())), preferred_element_type=jnp.float32)


def _ffn_kernel(n_up, prenorm, emit_next, *refs):
    x_ref, a_ref, wg_ref, wu_ref, wd_ref, post_g_ref = refs[:6]
    refs = refs[6:]
    next_g_ref = None
    if emit_next:
        next_g_ref, refs = refs[0], refs[1:]
    o_ref, refs = refs[0], refs[1:]
    on_ref = None
    if emit_next:
        on_ref, refs = refs[0], refs[1:]
    if prenorm:
        xn_ref, refs = refs[0], refs[1:]
    else:
        xn_ref = a_ref
    h_ref, f_ref = refs
    j = pl.program_id(1)

    if prenorm:
        @pl.when(j == 0)
        def _():
            xn_ref[...] = _rms(x_ref[...], a_ref[...]).astype(xn_ref.dtype)

    @pl.when(j < n_up)
    def _():
        xn = xn_ref[...]
        g = _dot(xn, wg_ref[...])
        u = _dot(xn, wu_ref[...])
        h_ref[j] = (g * jax.nn.sigmoid(g) * u).astype(h_ref.dtype)

    @pl.when(j >= n_up)
    def _():
        h = jnp.concatenate([h_ref[k] for k in range(n_up)], axis=1)
        f_ref[j - n_up] = _dot(h, wd_ref[...])

    @pl.when(j == pl.num_programs(1) - 1)
    def _():
        f = jnp.concatenate([f_ref[n] for n in range(f_ref.shape[0])], axis=1)
        y = x_ref[...] + FFN_RES_SCALE * _rms(f, post_g_ref[...])
        o_ref[...] = y
        if emit_next:
            on_ref[...] = _rms(y, next_g_ref[...]).astype(on_ref.dtype)


def _ffn(x, xn, pre_g, wg, wu, wd, post_g, next_g):
    T, D = x.shape
    F = wg.shape[1]
    tm, tf, tn = FFN_TM, FFN_TF, FFN_TN
    assert T % tm == 0 and F % tf == 0 and D % tn == 0
    assert (xn is None) != (pre_g is None)
    prenorm, emit_next = xn is None, next_g is not None
    n_up, n_down = F // tf, D // tn
    row_spec = pl.BlockSpec((tm, D), lambda i, j: (i, 0))
    gain_spec = pl.BlockSpec((1, D), lambda i, j: (0, 0))
    up_tiles = lambda w: w.reshape(D, n_up, tf).transpose(1, 0, 2)
    up_spec = pl.BlockSpec((None, D, tf), lambda i, j: (jnp.minimum(j, n_up - 1), 0, 0))
    in_specs = [row_spec, gain_spec if prenorm else row_spec, up_spec, up_spec,
                pl.BlockSpec((None, F, tn), lambda i, j: (jnp.maximum(j - n_up, 0), 0, 0)), gain_spec]
    args = [x, pre_g if prenorm else xn, up_tiles(wg), up_tiles(wu),
            wd.reshape(F, n_down, tn).transpose(1, 0, 2), post_g]
    out_specs, out_shape = [row_spec], [jax.ShapeDtypeStruct((T, D), jnp.float32)]
    scratch = [pltpu.VMEM((n_up, tm, tf), jnp.bfloat16), pltpu.VMEM((n_down, tm, tn), jnp.float32)]
    if emit_next:
        in_specs.append(gain_spec)
        args.append(next_g)
        out_specs.append(row_spec)
        out_shape.append(jax.ShapeDtypeStruct((T, D), jnp.bfloat16))
    if prenorm:
        scratch.insert(0, pltpu.VMEM((tm, D), jnp.bfloat16))
    outs = pl.pallas_call(
        functools.partial(_ffn_kernel, n_up, prenorm, emit_next),
        grid=(T // tm, n_up + n_down),
        in_specs=in_specs,
        out_specs=out_specs,
        out_shape=out_shape,
        scratch_shapes=scratch,
        compiler_params=pltpu.CompilerParams(
            dimension_semantics=("parallel", "arbitrary"), vmem_limit_bytes=VMEM_LIMIT),
        name="ffn",
    )(*args)
    return (outs[0], outs[1]) if emit_next else (outs[0], None)


def _proj_kernel(n_rope_tiles, xn_ref, w_ref, cos_ref, sa_ref, sb_ref, o_ref):
    j = pl.program_id(1)

    def tile(rope):
        xn = xn_ref[...]
        ys = [_dot(xn, w_ref[:, c:c + MXU_COLS]) for c in range(0, w_ref.shape[1], MXU_COLS)]
        for n, y in enumerate(ys):
            for h in range(0, MXU_COLS, HEAD_DIM):
                yh = y[:, h:h + HEAD_DIM]
                if rope:
                    yh = (yh * cos_ref[...] + pltpu.roll(yh, HEAD_DIM - ROT_DIM // 2, 1) * sa_ref[...]
                          + pltpu.roll(yh, ROT_DIM // 2, 1) * sb_ref[...])
                o_ref[:, n * MXU_COLS + h:n * MXU_COLS + h + HEAD_DIM] = yh.astype(o_ref.dtype)

    pl.when(j < n_rope_tiles)(functools.partial(tile, True))
    pl.when(j >= n_rope_tiles)(functools.partial(tile, False))


def _proj(xn, w, tables, seq, n_rope_cols):
    T, D = xn.shape
    N = w.shape[1]
    tm, tn = PROJ_TM, PROJ_TN
    assert T % tm == 0 and N % tn == 0 and seq % tm == 0 and n_rope_cols % tn == 0 and tn % MXU_COLS == 0
    pos_tiles = seq // tm
    tab_spec = pl.BlockSpec((tm, HEAD_DIM), lambda i, j: (i % pos_tiles, 0))
    return pl.pallas_call(
        functools.partial(_proj_kernel, n_rope_cols // tn),
        grid=(T // tm, N // tn),
        in_specs=[
            pl.BlockSpec((tm, D), lambda i, j: (i, 0)),
            pl.BlockSpec((D, tn), lambda i, j: (0, j)),
            tab_spec, tab_spec, tab_spec,
        ],
        out_specs=pl.BlockSpec((tm, tn), lambda i, j: (i, j)),
        out_shape=jax.ShapeDtypeStruct((T, N), jnp.bfloat16),
        compiler_params=pltpu.CompilerParams(
            dimension_semantics=("parallel", "arbitrary"), vmem_limit_bytes=VMEM_LIMIT),
        name="proj",
    )(xn, w, *tables)


def _rope_tables(seq):
    half = ROT_DIM // 2
    inv_freq = ROPE_THETA ** (-jnp.arange(0, ROT_DIM, 2, dtype=jnp.float32) / ROT_DIM)
    ang = jnp.arange(seq, dtype=jnp.float32)[:, None] * inv_freq[None, :]
    cos, sin = jnp.cos(ang), jnp.sin(ang)
    zeros = jnp.zeros((seq, HEAD_DIM - half), jnp.float32)
    cos_t = jnp.concatenate([cos, cos, jnp.ones((seq, HEAD_DIM - ROT_DIM), jnp.float32)], axis=1)
    sa_t = jnp.concatenate([-sin, zeros], axis=1)
    sb_t = jnp.concatenate([jnp.zeros((seq, half), jnp.float32), sin,
                            jnp.zeros((seq, HEAD_DIM - ROT_DIM), jnp.float32)], axis=1)
    return cos_t, sa_t, sb_t


def _moba_kernel(n_group, q_ref, k_ref, v_ref, o_ref, kaug_ref, kmh_ref, kml_ref):
    qi = pl.program_id(2)
    blk = MOBA_BLOCK
    S = k_ref.shape[1]
    nb = S // blk
    exp2_scale = HEAD_DIM ** -0.5 * LOG2_E
    heads = [slice(g * HEAD_DIM, (g + 1) * HEAD_DIM) for g in range(n_group)]

    @pl.when(qi == 0)
    def _():
        key_block = lax.shift_right_logical(lax.broadcasted_iota(jnp.int32, (S, LANES), 0),
                                            blk.bit_length() - 1)
        one_hot = (key_block == lax.broadcasted_iota(jnp.int32, (S, LANES), 1)).astype(jnp.bfloat16)
        kmh_ref[...] = jnp.zeros_like(kmh_ref)
        kml_ref[...] = jnp.zeros_like(kml_ref)
        for g, hs in enumerate(heads):
            kaug_ref[g, :, :HEAD_DIM] = k_ref[0, :, hs]
            kaug_ref[g, :, HEAD_DIM:] = one_hot
            for n in range(nb):
                km = jnp.mean(k_ref[0, n * blk:(n + 1) * blk, hs].astype(jnp.float32), axis=0, keepdims=True)
                hi = km.astype(jnp.bfloat16)
                kmh_ref[g, n:n + 1, :] = hi
                kml_ref[g, n:n + 1, :] = (km - hi.astype(jnp.float32)).astype(jnp.bfloat16)

    lane = lax.broadcasted_iota(jnp.int32, (blk, LANES), 1)
    lane_f = lane.astype(jnp.float32)
    past = lane < qi
    q_augs = []
    for g, hs in enumerate(heads):
        q = q_ref[0, :, hs]
        gate = _dot_nt(q, kmh_ref[g]) + _dot_nt(q, kml_ref[g])
        gate = jnp.where(past, gate, -jnp.inf)
        attend = lane == qi
        for _ in range(MOBA_TOPK):
            m = jnp.max(gate, axis=1, keepdims=True)
            first = jnp.min(jnp.where(gate == m, lane_f, float(LANES)), axis=1, keepdims=True)
            hit = lane_f == first
            attend = attend | (hit & past)
            gate = jnp.where(hit, -jnp.inf, gate)
        penalty = jnp.where(attend, 0.0, NEG_BIG).astype(jnp.bfloat16)
        q_augs.append(jnp.concatenate([q, penalty], axis=1))

    def scores(kj):
        start = pl.multiple_of(kj * blk, blk)
        return start, [_dot_nt(q_augs[g], kaug_ref[g, pl.ds(start, blk), :]) for g in range(n_group)]

    row = lax.broadcasted_iota(jnp.int32, (blk, blk), 0)
    col = lax.broadcasted_iota(jnp.int32, (blk, blk), 1)
    start, ss = scores(qi)
    carry = []
    for g, hs in enumerate(heads):
        s = jnp.where(col <= row, ss[g], NEG_BIG)
        m0 = jnp.max(s, axis=1, keepdims=True)
        p = jnp.exp2((s - m0) * exp2_scale)
        l0 = jnp.sum(p, axis=1, keepdims=True)
        carry.append((m0, l0, _dot(p.astype(jnp.bfloat16), v_ref[0, pl.ds(start, blk), hs])))

    def body(kj, carry):
        start, ss = scores(kj)
        out = []
        for g, hs in enumerate(heads):
            m_prev, l_prev, acc_prev = carry[g]
            m_new = jnp.maximum(m_prev, jnp.max(ss[g], axis=1, keepdims=True))
            alpha = jnp.exp2((m_prev - m_new) * exp2_scale)
            p = jnp.exp2((ss[g] - m_new) * exp2_scale)
            l_new = alpha * l_prev + jnp.sum(p, axis=1, keepdims=True)
            acc_new = alpha * acc_prev + _dot(p.astype(jnp.bfloat16), v_ref[0, pl.ds(start, blk), hs])
            out.append((m_new, l_new, acc_new))
        return tuple(out)

    carry = lax.fori_loop(0, qi, body, tuple(carry))
    for g, hs in enumerate(heads):
        _, l_fin, acc = carry[g]
        o_ref[0, :, hs] = (acc / l_fin).astype(o_ref.dtype)


def _moba(qkv, n_heads, q_col, k_col, v_col):
    B, S, _ = qkv.shape
    blk, G = MOBA_BLOCK, MOBA_GROUP
    assert S % blk == 0 and S // blk <= LANES and blk & (blk - 1) == 0
    assert n_heads % G == 0 and q_col % G == 0 and k_col % G == 0 and v_col % G == 0
    W = G * HEAD_DIM
    return pl.pallas_call(
        functools.partial(_moba_kernel, G),
        grid=(B, n_heads // G, S // blk),
        in_specs=[
            pl.BlockSpec((1, blk, W), lambda b, h, i: (b, i, q_col // G + h)),
            pl.BlockSpec((1, S, W), lambda b, h, i: (b, 0, k_col // G + h)),
            pl.BlockSpec((1, S, W), lambda b, h, i: (b, 0, v_col // G + h)),
        ],
        out_specs=pl.BlockSpec((1, blk, W), lambda b, h, i: (b, i, h)),
        out_shape=jax.ShapeDtypeStruct((B, S, n_heads * HEAD_DIM), jnp.bfloat16),
        scratch_shapes=[pltpu.VMEM((G, S, HEAD_DIM + LANES), jnp.bfloat16),
                        pltpu.VMEM((G, LANES, HEAD_DIM), jnp.bfloat16),
                        pltpu.VMEM((G, LANES, HEAD_DIM), jnp.bfloat16)],
        compiler_params=pltpu.CompilerParams(
            dimension_semantics=("parallel", "parallel", "arbitrary")),
        name="moba",
    )(qkv, qkv, qkv)


def _sb_kernel(n_group, q_ref, k_ref, v_ref, tri_ref, o_ref):
    qi = pl.program_id(2)
    blk = SB_BLOCK
    scale = HEAD_DIM ** -0.5
    neg_tri = tri_ref[...]
    heads = [slice(g * HEAD_DIM, (g + 1) * HEAD_DIM) for g in range(n_group)]
    qs = [q_ref[0, :, hs] for hs in heads]

    def block(kj, carry, diagonal):
        start = pl.multiple_of(kj * blk, blk)
        if diagonal:
            row = lax.broadcasted_iota(jnp.int32, (blk, blk), 0)
            col = lax.broadcasted_iota(jnp.int32, (blk, blk), 1)
            causal = col < row
        zs = [_dot_nt(qs[g], k_ref[0, pl.ds(start, blk), hs]) * scale for g, hs in enumerate(heads)]
        log_betas, row_sums, sticks = [], [], []
        for g in range(n_group):
            z = zs[g]
            softplus = jnp.maximum(z, 0.0) + jnp.log(1.0 + jnp.exp2(jnp.abs(z) * -LOG2_E))
            log_betas.append(z - softplus)
            if diagonal:
                softplus = jnp.where(causal, softplus, 0.0)
            row_sums.append(jnp.sum(softplus, axis=1, keepdims=True))
            sticks.append(_dot(softplus.astype(jnp.bfloat16), neg_tri))
        out = []
        for g, hs in enumerate(heads):
            rest, acc = carry[g]
            a = jnp.exp(log_betas[g] + sticks[g] + rest)
            if diagonal:
                a = jnp.where(causal, a, 0.0)
            acc = acc + _dot(a.astype(jnp.bfloat16), v_ref[0, pl.ds(start, blk), hs])
            out.append((rest - row_sums[g], acc))
        return tuple(out)

    init = tuple((jnp.zeros((blk, 1), jnp.float32), jnp.zeros((blk, HEAD_DIM), jnp.float32))
                 for _ in heads)
    carry = block(qi, init, True)
    carry = lax.fori_loop(0, qi, lambda step, c: block(qi - 1 - step, c, False), carry)
    for g, hs in enumerate(heads):
        o_ref[0, :, hs] = carry[g][1].astype(o_ref.dtype)


def _sb(qkv, n_heads, q_col, k_col, v_col):
    B, S, _ = qkv.shape
    blk, G = SB_BLOCK, SB_GROUP
    assert S % blk == 0 and n_heads % G == 0 and q_col % G == 0 and k_col % G == 0 and v_col % G == 0
    idx = jnp.arange(blk)
    neg_tri = -(idx[:, None] > idx[None, :]).astype(jnp.bfloat16)
    W = G * HEAD_DIM
    return pl.pallas_call(
        functools.partial(_sb_kernel, G),
        grid=(B, n_heads // G, S // blk),
        in_specs=[
            pl.BlockSpec((1, blk, W), lambda b, h, i: (b, i, q_col // G + h)),
            pl.BlockSpec((1, S, W), lambda b, h, i: (b, 0, k_col // G + h)),
            pl.BlockSpec((1, S, W), lambda b, h, i: (b, 0, v_col // G + h)),
            pl.BlockSpec((blk, blk), lambda b, h, i: (0, 0)),
        ],
        out_specs=pl.BlockSpec((1, blk, W), lambda b, h, i: (b, i, h)),
        out_shape=jax.ShapeDtypeStruct((B, S, n_heads * HEAD_DIM), jnp.bfloat16),
        compiler_params=pltpu.CompilerParams(
            dimension_semantics=("parallel", "parallel", "arbitrary")),
        name="sb",
    )(qkv, qkv, qkv, neg_tri)


def _out_kernel(x_ref, oa_ref, ob_ref, ga_ref, gb_ref, wa_ref, wb_ref, g_ref, next_g_ref, o_ref, on_ref):
    na = _rms(oa_ref[...].astype(jnp.float32), ga_ref[...]).astype(jnp.bfloat16)
    nb = _rms(ob_ref[...].astype(jnp.float32), gb_ref[...]).astype(jnp.bfloat16)
    y = x_ref[...] + _rms(_dot(na, wa_ref[...]) + _dot(nb, wb_ref[...]), g_ref[...])
    o_ref[...] = y
    on_ref[...] = _rms(y, next_g_ref[...]).astype(on_ref.dtype)


def _out(x, oa, ob, ga, gb, w, g, next_g):
    T, D = x.shape
    Wa, Wb = oa.shape[1], ob.shape[1]
    assert Wa == Wb and w.shape[0] == Wa + Wb
    tm = OUT_TM
    assert T % tm == 0
    return pl.pallas_call(
        _out_kernel,
        grid=(T // tm,),
        in_specs=[
            pl.BlockSpec((tm, D), lambda i: (i, 0)),
            pl.BlockSpec((tm, Wa), lambda i: (i, 0)),
            pl.BlockSpec((tm, Wb), lambda i: (i, 0)),
            pl.BlockSpec((1, Wa), lambda i: (0, 0)),
            pl.BlockSpec((1, Wb), lambda i: (0, 0)),
            pl.BlockSpec((Wa, D), lambda i: (0, 0)),
            pl.BlockSpec((Wb, D), lambda i: (1, 0)),
            pl.BlockSpec((1, D), lambda i: (0, 0)),
            pl.BlockSpec((1, D), lambda i: (0, 0)),
        ],
        out_specs=[pl.BlockSpec((tm, D), lambda i: (i, 0)), pl.BlockSpec((tm, D), lambda i: (i, 0))],
        out_shape=[jax.ShapeDtypeStruct((T, D), jnp.float32), jax.ShapeDtypeStruct((T, D), jnp.bfloat16)],
        compiler_params=pltpu.CompilerParams(
            dimension_semantics=("parallel",), vmem_limit_bytes=VMEM_LIMIT),
        name="out_proj",
    )(x, oa, ob, ga, gb, w, w, g, next_g)


def kernel(x, ffn1_pre_g, ffn1_w_gate, ffn1_w_up, ffn1_w_down, ffn1_post_g, mix_pre_g, w_in, moba_out_g, sb_out_g, w_out, mix_post_g, ffn2_pre_g, ffn2_w_gate, ffn2_w_up, ffn2_w_down, ffn2_post_g):
    B, S, D = x.shape
    depth = w_in.shape[0]
    w_moba = moba_out_g.shape[1]
    w_sb = sb_out_g.shape[1]
    h_moba, h_sb = w_moba // HEAD_DIM, w_sb // HEAD_DIM
    bf = lambda w: w.astype(jnp.bfloat16)
    tables = _rope_tables(S)

    xt = x.reshape(B * S, D)
    for l in range(depth):
        xt, xn = _ffn(xt, None, ffn1_pre_g[l:l + 1], bf(ffn1_w_gate[l]), bf(ffn1_w_up[l]),
                      bf(ffn1_w_down[l]), ffn1_post_g[l:l + 1], mix_pre_g[l:l + 1])
        qkv = _proj(xn, bf(w_in[l]), tables, S, 2 * w_moba).reshape(B, S, -1)
        o_a = _moba(qkv, h_moba, 0, h_moba, 2 * h_moba)
        o_b = _sb(qkv, h_sb, 3 * h_moba, 3 * h_moba + h_sb, 3 * h_moba + 2 * h_sb)
        xt, xn = _out(xt, o_a.reshape(B * S, w_moba), o_b.reshape(B * S, w_sb),
                      moba_out_g[l:l + 1], sb_out_g[l:l + 1], bf(w_out[l]), mix_post_g[l:l + 1],
                      ffn2_pre_g[l:l + 1])
        xt, _ = _ffn(xt, xn, None, bf(ffn2_w_gate[l]), bf(ffn2_w_up[l]), bf(ffn2_w_down[l]),
                     ffn2_post_g[l:l + 1], None)
    return xt.reshape(B, S, D)
```

```python
import functools

import jax
import jax.numpy as jnp
from jax import lax
from jax.experimental import pallas as pl
from jax.experimental.pallas import tpu as pltpu

HEAD_DIM = 128
MOBA_BLOCK = 256
MOBA_TOPK = 3
ROPE_THETA = 500000.0
ROT_DIM = HEAD_DIM // 4
RMS_EPS = 1e-6
FFN_RES_SCALE = 0.5

LANES = 128
LOG2_E = 1.4426950408889634
NEG_BIG = -1e30
MOBA_GROUP = 4
SB_BLOCK = 256
SB_GROUP = 4
VMEM_LIMIT = 56 * 1024 * 1024

NORM_TM = 1024
UP_TM, UP_TF = 2048, 512
DOWN_TM = 256
PROJ_TM, PROJ_TN = 1024, 1024
MXU_COLS = 256
OUT_TM = 512


def _rms(xf, g):
    return xf * lax.rsqrt(jnp.mean(xf * xf, axis=-1, keepdims=True) + RMS_EPS) * g


def _dot(a, b):
    return jnp.dot(a, b, preferred_element_type=jnp.float32)


def _dot_nt(a, b):
    return lax.dot_general(a, b, (((1,), (1,)), ((), ())), preferred_element_type=jnp.float32)


def _prenorm_kernel(x_ref, g_ref, o_ref):
    o_ref[...] = _rms(x_ref[...], g_ref[...]).astype(o_ref.dtype)


def _prenorm(x, g):
    T, D = x.shape
    tm = NORM_TM
    assert T % tm == 0
    return pl.pallas_call(
        _prenorm_kernel,
        grid=(T // tm,),
        in_specs=[pl.BlockSpec((tm, D), lambda i: (i, 0)), pl.BlockSpec((1, D), lambda i: (0, 0))],
        out_specs=pl.BlockSpec((tm, D), lambda i: (i, 0)),
        out_shape=jax.ShapeDtypeStruct((T, D), jnp.bfloat16),
        compiler_params=pltpu.CompilerParams(
            dimension_semantics=("parallel",), vmem_limit_bytes=VMEM_LIMIT),
        name="prenorm",
    )(x, g)


def _ffn_up_kernel(xn_ref, wg_ref, wu_ref, h_ref):
    xn = xn_ref[...]
    cols = [slice(c, c + MXU_COLS) for c in range(0, h_ref.shape[1], MXU_COLS)]
    gu = [(_dot(xn, wg_ref[:, cs].astype(jnp.bfloat16)), _dot(xn, wu_ref[:, cs].astype(jnp.bfloat16)))
          for cs in cols]
    for cs, (g, u) in zip(cols, gu):
        h_ref[:, cs] = (g * jax.nn.sigmoid(g) * u).astype(h_ref.dtype)


def _ffn_up(xn, wg, wu):
    T, D = xn.shape
    F = wg.shape[1]
    tm, tf = UP_TM, UP_TF
    assert T % tm == 0 and F % tf == 0 and tf % MXU_COLS == 0
    w_spec = pl.BlockSpec((D, tf), lambda i, j: (0, j))
    return pl.pallas_call(
        _ffn_up_kernel,
        grid=(T // tm, F // tf),
        in_specs=[pl.BlockSpec((tm, D), lambda i, j: (i, 0)), w_spec, w_spec],
        out_specs=pl.BlockSpec((tm, tf), lambda i, j: (i, j)),
        out_shape=jax.ShapeDtypeStruct((T, F), jnp.bfloat16),
        compiler_params=pltpu.CompilerParams(
            dimension_semantics=("parallel", "arbitrary"), vmem_limit_bytes=VMEM_LIMIT),
        name="ffn_up",
    )(xn, wg, wu)


def _ffn_down_kernel(emit_next, h_ref, wd_ref, x_ref, post_g_ref, *refs):
    y = x_ref[...] + FFN_RES_SCALE * _rms(_dot(h_ref[...], wd_ref[...]), post_g_ref[...])
    if emit_next:
        next_g_ref, o_ref, on_ref = refs
        on_ref[...] = _rms(y, next_g_ref[...]).astype(on_ref.dtype)
    else:
        o_ref, = refs
    o_ref[...] = y


def _ffn_down(h, wd, x, post_g, next_g):
    T, F = h.shape
    D = wd.shape[1]
    tm = DOWN_TM
    assert T % tm == 0
    emit_next = next_g is not None
    row_spec = pl.BlockSpec((tm, D), lambda i: (i, 0))
    gain_spec = pl.BlockSpec((1, D), lambda i: (0, 0))
    in_specs = [pl.BlockSpec((tm, F), lambda i: (i, 0)),
                pl.BlockSpec((F, D), lambda i: (0, 0), pipeline_mode=pl.Buffered(1)),
                row_spec, gain_spec]
    args = [h, wd, x, post_g]
    out_specs, out_shape = [row_spec], [jax.ShapeDtypeStruct((T, D), jnp.float32)]
    if emit_next:
        in_specs.append(gain_spec)
        args.append(next_g)
        out_specs.append(row_spec)
        out_shape.append(jax.ShapeDtypeStruct((T, D), jnp.bfloat16))
    outs = pl.pallas_call(
        functools.partial(_ffn_down_kernel, emit_next),
        grid=(T // tm,),
        in_specs=in_specs,
        out_specs=out_specs,
        out_shape=out_shape,
        compiler_params=pltpu.CompilerParams(
            dimension_semantics=("parallel",), vmem_limit_bytes=VMEM_LIMIT),
        name="ffn_down",
    )(*args)
    return (outs[0], outs[1]) if emit_next else (outs[0], None)


def _proj_kernel(n_rope_tiles, xn_ref, w_ref, cos_ref, sa_ref, sb_ref, o_ref):
    j = pl.program_id(1)

    def tile(rope):
        xn = xn_ref[...]
        ys = [_dot(xn, w_ref[:, c:c + MXU_COLS]) for c in range(0, w_ref.shape[1], MXU_COLS)]
        for n, y in enumerate(ys):
            for h in range(0, MXU_COLS, HEAD_DIM):
                yh = y[:, h:h + HEAD_DIM]
                if rope:
                    yh = (yh * cos_ref[...] + pltpu.roll(yh, HEAD_DIM - ROT_DIM // 2, 1) * sa_ref[...]
                          + pltpu.roll(yh, ROT_DIM // 2, 1) * sb_ref[...])
                o_ref[:, n * MXU_COLS + h:n * MXU_COLS + h + HEAD_DIM] = yh.astype(o_ref.dtype)

    pl.when(j < n_rope_tiles)(functools.partial(tile, True))
    pl.when(j >= n_rope_tiles)(functools.partial(tile, False))


def _proj(xn, w, tables, seq, n_rope_cols):
    T, D = xn.shape
    N = w.shape[1]
    tm, tn = PROJ_TM, PROJ_TN
    assert T % tm == 0 and N % tn == 0 and seq % tm == 0 and n_rope_cols % tn == 0 and tn % MXU_COLS == 0
    pos_tiles = seq // tm
    tab_spec = pl.BlockSpec((tm, HEAD_DIM), lambda i, j: (i % pos_tiles, 0))
    return pl.pallas_call(
        functools.partial(_proj_kernel, n_rope_cols // tn),
        grid=(T // tm, N // tn),
        in_specs=[
            pl.BlockSpec((tm, D), lambda i, j: (i, 0)),
            pl.BlockSpec((D, tn), lambda i, j: (0, j)),
            tab_spec, tab_spec, tab_spec,
        ],
        out_specs=pl.BlockSpec((tm, tn), lambda i, j: (i, j)),
        out_shape=jax.ShapeDtypeStruct((T, N), jnp.bfloat16),
        compiler_params=pltpu.CompilerParams(
            dimension_semantics=("parallel", "arbitrary"), vmem_limit_bytes=VMEM_LIMIT),
        name="proj",
    )(xn, w, *tables)


def _rope_tables(seq):
    half = ROT_DIM // 2
    inv_freq = ROPE_THETA ** (-jnp.arange(0, ROT_DIM, 2, dtype=jnp.float32) / ROT_DIM)
    ang = jnp.arange(seq, dtype=jnp.float32)[:, None] * inv_freq[None, :]
    cos, sin = jnp.cos(ang), jnp.sin(ang)
    zeros = jnp.zeros((seq, HEAD_DIM - half), jnp.float32)
    cos_t = jnp.concatenate([cos, cos, jnp.ones((seq, HEAD_DIM - ROT_DIM), jnp.float32)], axis=1)
    sa_t = jnp.concatenate([-sin, zeros], axis=1)
    sb_t = jnp.concatenate([jnp.zeros((seq, half), jnp.float32), sin,
                            jnp.zeros((seq, HEAD_DIM - ROT_DIM), jnp.float32)], axis=1)
    return cos_t, sa_t, sb_t


def _moba_kernel(n_group, q_ref, k_ref, v_ref, o_ref, kaug_ref, kmh_ref, kml_ref):
    qi = pl.program_id(2)
    blk = MOBA_BLOCK
    S = k_ref.shape[1]
    nb = S // blk
    exp2_scale = HEAD_DIM ** -0.5 * LOG2_E
    heads = [slice(g * HEAD_DIM, (g + 1) * HEAD_DIM) for g in range(n_group)]

    @pl.when(qi == 0)
    def _():
        key_block = lax.shift_right_logical(lax.broadcasted_iota(jnp.int32, (S, LANES), 0),
                                            blk.bit_length() - 1)
        one_hot = (key_block == lax.broadcasted_iota(jnp.int32, (S, LANES), 1)).astype(jnp.bfloat16)
        kmh_ref[...] = jnp.zeros_like(kmh_ref)
        kml_ref[...] = jnp.zeros_like(kml_ref)
        for g, hs in enumerate(heads):
            kaug_ref[g, :, :HEAD_DIM] = k_ref[0, :, hs]
            kaug_ref[g, :, HEAD_DIM:] = one_hot
            for n in range(nb):
                km = jnp.mean(k_ref[0, n * blk:(n + 1) * blk, hs].astype(jnp.float32), axis=0, keepdims=True)
                hi = km.astype(jnp.bfloat16)
                kmh_ref[g, n:n + 1, :] = hi
                kml_ref[g, n:n + 1, :] = (km - hi.astype(jnp.float32)).astype(jnp.bfloat16)

    lane = lax.broadcasted_iota(jnp.int32, (blk, LANES), 1)
    lane_f = lane.astype(jnp.float32)
    past = lane < qi
    q_augs = []
    for g, hs in enumerate(heads):
        q = q_ref[0, :, hs]
        gate = _dot_nt(q, kmh_ref[g]) + _dot_nt(q, kml_ref[g])
        gate = jnp.where(past, gate, -jnp.inf)
        attend = lane == qi
        for _ in range(MOBA_TOPK):
            m = jnp.max(gate, axis=1, keepdims=True)
            first = jnp.min(jnp.where(gate == m, lane_f, float(LANES)), axis=1, keepdims=True)
            hit = lane_f == first
            attend = attend | (hit & past)
            gate = jnp.where(hit, -jnp.inf, gate)
        penalty = jnp.where(attend, 0.0, NEG_BIG).astype(jnp.bfloat16)
        q_augs.append(jnp.concatenate([q, penalty], axis=1))

    def scores(kj):
        start = pl.multiple_of(kj * blk, blk)
        return start, [_dot_nt(q_augs[g], kaug_ref[g, pl.ds(start, blk), :]) for g in range(n_group)]

    row = lax.broadcasted_iota(jnp.int32, (blk, blk), 0)
    col = lax.broadcasted_iota(jnp.int32, (blk, blk), 1)
    start, ss = scores(qi)
    carry = []
    for g, hs in enumerate(heads):
        s = jnp.where(col <= row, ss[g], NEG_BIG)
        m0 = jnp.max(s, axis=1, keepdims=True)
        p = jnp.exp2((s - m0) * exp2_scale)
        l0 = jnp.sum(p, axis=1, keepdims=True)
        carry.append((m0, l0, _dot(p.astype(jnp.bfloat16), v_ref[0, pl.ds(start, blk), hs])))

    def body(kj, carry):
        start, ss = scores(kj)
        out = []
        for g, hs in enumerate(heads):
            m_prev, l_prev, acc_prev = carry[g]
            m_new = jnp.maximum(m_prev, jnp.max(ss[g], axis=1, keepdims=True))
            alpha = jnp.exp2((m_prev - m_new) * exp2_scale)
            p = jnp.exp2((ss[g] - m_new) * exp2_scale)
            l_new = alpha * l_prev + jnp.sum(p, axis=1, keepdims=True)
            acc_new = alpha * acc_prev + _dot(p.astype(jnp.bfloat16), v_ref[0, pl.ds(start, blk), hs])
            out.append((m_new, l_new, acc_new))
        return tuple(out)

    carry = lax.fori_loop(0, qi, body, tuple(carry))
    for g, hs in enumerate(heads):
        _, l_fin, acc = carry[g]
        o_ref[0, :, hs] = (acc / l_fin).astype(o_ref.dtype)


def _moba(qkv, n_heads, q_col, k_col, v_col):
    B, S, _ = qkv.shape
    blk, G = MOBA_BLOCK, MOBA_GROUP
    assert S % blk == 0 and S // blk <= LANES and blk & (blk - 1) == 0
    assert n_heads % G == 0 and q_col % G == 0 and k_col % G == 0 and v_col % G == 0
    W = G * HEAD_DIM
    return pl.pallas_call(
        functools.partial(_moba_kernel, G),
        grid=(B, n_heads // G, S // blk),
        in_specs=[
            pl.BlockSpec((1, blk, W), lambda b, h, i: (b, i, q_col // G + h)),
            pl.BlockSpec((1, S, W), lambda b, h, i: (b, 0, k_col // G + h)),
            pl.BlockSpec((1, S, W), lambda b, h, i: (b, 0, v_col // G + h)),
        ],
        out_specs=pl.BlockSpec((1, blk, W), lambda b, h, i: (b, i, h)),
        out_shape=jax.ShapeDtypeStruct((B, S, n_heads * HEAD_DIM), jnp.bfloat16),
        scratch_shapes=[pltpu.VMEM((G, S, HEAD_DIM + LANES), jnp.bfloat16),
                        pltpu.VMEM((G, LANES, HEAD_DIM), jnp.bfloat16),
                        pltpu.VMEM((G, LANES, HEAD_DIM), jnp.bfloat16)],
        compiler_params=pltpu.CompilerParams(
            dimension_semantics=("parallel", "parallel", "arbitrary")),
        name="moba",
    )(qkv, qkv, qkv)


def _sb_kernel(n_group, q_ref, k_ref, v_ref, tri_ref, o_ref):
    qi = pl.program_id(2)
    blk = SB_BLOCK
    scale = HEAD_DIM ** -0.5
    neg_tri = tri_ref[...]
    heads = [slice(g * HEAD_DIM, (g + 1) * HEAD_DIM) for g in range(n_group)]
    qs = [q_ref[0, :, hs] for hs in heads]

    def block(kj, carry, diagonal):
        start = pl.multiple_of(kj * blk, blk)
        if diagonal:
            row = lax.broadcasted_iota(jnp.int32, (blk, blk), 0)
            col = lax.broadcasted_iota(jnp.int32, (blk, blk), 1)
            causal = col < row
        zs = [_dot_nt(qs[g], k_ref[0, pl.ds(start, blk), hs]) * scale for g, hs in enumerate(heads)]
        log_betas, row_sums, sticks = [], [], []
        for g in range(n_group):
            z = zs[g]
            softplus = jnp.maximum(z, 0.0) + jnp.log(1.0 + jnp.exp2(jnp.abs(z) * -LOG2_E))
            log_betas.append(z - softplus)
            if diagonal:
                softplus = jnp.where(causal, softplus, 0.0)
            row_sums.append(jnp.sum(softplus, axis=1, keepdims=True))
            sticks.append(_dot(softplus.astype(jnp.bfloat16), neg_tri))
        out = []
        for g, hs in enumerate(heads):
            rest, acc = carry[g]
            a = jnp.exp(log_betas[g] + sticks[g] + rest)
            if diagonal:
                a = jnp.where(causal, a, 0.0)
            acc = acc + _dot(a.astype(jnp.bfloat16), v_ref[0, pl.ds(start, blk), hs])
            out.append((rest - row_sums[g], acc))
        return tuple(out)

    init = tuple((jnp.zeros((blk, 1), jnp.float32), jnp.zeros((blk, HEAD_DIM), jnp.float32))
                 for _ in heads)
    carry = block(qi, init, True)
    carry = lax.fori_loop(0, qi, lambda step, c: block(qi - 1 - step, c, False), carry)
    for g, hs in enumerate(heads):
        o_ref[0, :, hs] = carry[g][1].astype(o_ref.dtype)


def _sb(qkv, n_heads, q_col, k_col, v_col):
    B, S, _ = qkv.shape
    blk, G = SB_BLOCK, SB_GROUP
    assert S % blk == 0 and n_heads % G == 0 and q_col % G == 0 and k_col % G == 0 and v_col % G == 0
    idx = jnp.arange(blk)
    neg_tri = -(idx[:, None] > idx[None, :]).astype(jnp.bfloat16)
    W = G * HEAD_DIM
    return pl.pallas_call(
        functools.partial(_sb_kernel, G),
        grid=(B, n_heads // G, S // blk),
        in_specs=[
            pl.BlockSpec((1, blk, W), lambda b, h, i: (b, i, q_col // G + h)),
            pl.BlockSpec((1, S, W), lambda b, h, i: (b, 0, k_col // G + h)),
            pl.BlockSpec((1, S, W), lambda b, h, i: (b, 0, v_col // G + h)),
            pl.BlockSpec((blk, blk), lambda b, h, i: (0, 0)),
        ],
        out_specs=pl.BlockSpec((1, blk, W), lambda b, h, i: (b, i, h)),
        out_shape=jax.ShapeDtypeStruct((B, S, n_heads * HEAD_DIM), jnp.bfloat16),
        compiler_params=pltpu.CompilerParams(
            dimension_semantics=("parallel", "parallel", "arbitrary")),
        name="sb",
    )(qkv, qkv, qkv, neg_tri)


def _out_kernel(x_ref, oa_ref, ob_ref, ga_ref, gb_ref, wa_ref, wb_ref, g_ref, next_g_ref, o_ref, on_ref):
    na = _rms(oa_ref[...].astype(jnp.float32), ga_ref[...]).astype(jnp.bfloat16)
    nb = _rms(ob_ref[...].astype(jnp.float32), gb_ref[...]).astype(jnp.bfloat16)
    y = x_ref[...] + _rms(_dot(na, wa_ref[...]) + _dot(nb, wb_ref[...]), g_ref[...])
    o_ref[...] = y
    on_ref[...] = _rms(y, next_g_ref[...]).astype(on_ref.dtype)


def _out(x, oa, ob, ga, gb, w, g, next_g):
    T, D = x.shape
    Wa, Wb = oa.shape[1], ob.shape[1]
    assert Wa == Wb and w.shape[0] == Wa + Wb
    tm = OUT_TM
    assert T % tm == 0
    return pl.pallas_call(
        _out_kernel,
        grid=(T // tm,),
        in_specs=[
            pl.BlockSpec((tm, D), lambda i: (i, 0)),
            pl.BlockSpec((tm, Wa), lambda i: (i, 0)),
            pl.BlockSpec((tm, Wb), lambda i: (i, 0)),
            pl.BlockSpec((1, Wa), lambda i: (0, 0)),
            pl.BlockSpec((1, Wb), lambda i: (0, 0)),
            pl.BlockSpec((Wa, D), lambda i: (0, 0)),
            pl.BlockSpec((Wb, D), lambda i: (1, 0)),
            pl.BlockSpec((1, D), lambda i: (0, 0)),
            pl.BlockSpec((1, D), lambda i: (0, 0)),
        ],
        out_specs=[pl.BlockSpec((tm, D), lambda i: (i, 0)), pl.BlockSpec((tm, D), lambda i: (i, 0))],
        out_shape=[jax.ShapeDtypeStruct((T, D), jnp.float32), jax.ShapeDtypeStruct((T, D), jnp.bfloat16)],
        compiler_params=pltpu.CompilerParams(
            dimension_semantics=("parallel",), vmem_limit_bytes=VMEM_LIMIT),
        name="out_proj",
    )(x, oa, ob, ga, gb, w, w, g, next_g)


def kernel(x, ffn1_pre_g, ffn1_w_gate, ffn1_w_up, ffn1_w_down, ffn1_post_g, mix_pre_g, w_in, moba_out_g, sb_out_g, w_out, mix_post_g, ffn2_pre_g, ffn2_w_gate, ffn2_w_up, ffn2_w_down, ffn2_post_g):
    B, S, D = x.shape
    depth = w_in.shape[0]
    w_moba = moba_out_g.shape[1]
    w_sb = sb_out_g.shape[1]
    h_moba, h_sb = w_moba // HEAD_DIM, w_sb // HEAD_DIM
    bf = lambda w: w.astype(jnp.bfloat16)
    tables = _rope_tables(S)

    xt = x.reshape(B * S, D)
    for l in range(depth):
        h = _ffn_up(_prenorm(xt, ffn1_pre_g[l:l + 1]), ffn1_w_gate[l], ffn1_w_up[l])
        xt, xn = _ffn_down(h, bf(ffn1_w_down[l]), xt, ffn1_post_g[l:l + 1], mix_pre_g[l:l + 1])
        qkv = _proj(xn, bf(w_in[l]), tables, S, 2 * w_moba).reshape(B, S, -1)
        o_a = _moba(qkv, h_moba, 0, h_moba, 2 * h_moba)
        o_b = _sb(qkv, h_sb, 3 * h_moba, 3 * h_moba + h_sb, 3 * h_moba + 2 * h_sb)
        xt, xn = _out(xt, o_a.reshape(B * S, w_moba), o_b.reshape(B * S, w_sb),
                      moba_out_g[l:l + 1], sb_out_g[l:l + 1], bf(w_out[l]), mix_post_g[l:l + 1],
                      ffn2_pre_g[l:l + 1])
        h = _ffn_up(xn, ffn2_w_gate[l], ffn2_w_up[l])
        xt, _ = _ffn_down(h, bf(ffn2_w_down[l]), xt, ffn2_post_g[l:l + 1], None)
    return xt.reshape(B, S, D)
```

```python
import functools

import jax
import jax.numpy as jnp
from jax import lax
from jax.experimental import pallas as pl
from jax.experimental.pallas import tpu as pltpu

HEAD_DIM = 128
MOBA_BLOCK = 256
MOBA_TOPK = 3
ROPE_THETA = 500000.0
ROT_DIM = HEAD_DIM // 4
RMS_EPS = 1e-6
FFN_RES_SCALE = 0.5

LANES = 128
LOG2_E = 1.4426950408889634
NEG_BIG = -1e30
MOBA_GROUP = 8
SB_BLOCK = 256
SB_GROUP = 8
VMEM_LIMIT = 56 * 1024 * 1024

NORM_TM = 1024
UP_TM, UP_TF = 2048, 512
DOWN_TM = 256
PROJ_TM, PROJ_TN = 1024, 1024
MXU_COLS = 256
OUT_TM = 512


def _rms(xf, g):
    return xf * lax.rsqrt(jnp.mean(xf * xf, axis=-1, keepdims=True) + RMS_EPS) * g


def _dot(a, b):
    return jnp.dot(a, b, preferred_element_type=jnp.float32)


def _dot_nt(a, b):
    return lax.dot_general(a, b, (((1,), (1,)), ((), ())), preferred_element_type=jnp.float32)


def _prenorm_kernel(x_ref, g_ref, o_ref):
    o_ref[...] = _rms(x_ref[...], g_ref[...]).astype(o_ref.dtype)


def _prenorm(x, g):
    T, D = x.shape
    tm = NORM_TM
    assert T % tm == 0
    return pl.pallas_call(
        _prenorm_kernel,
        grid=(T // tm,),
        in_specs=[pl.BlockSpec((tm, D), lambda i: (i, 0)), pl.BlockSpec((1, D), lambda i: (0, 0))],
        out_specs=pl.BlockSpec((tm, D), lambda i: (i, 0)),
        out_shape=jax.ShapeDtypeStruct((T, D), jnp.bfloat16),
        compiler_params=pltpu.CompilerParams(
            dimension_semantics=("parallel",), vmem_limit_bytes=VMEM_LIMIT),
        name="prenorm",
    )(x, g)


def _ffn_up_kernel(xn_ref, wg_ref, wu_ref, h_ref):
    xn = xn_ref[...]
    cols = [slice(c, c + MXU_COLS) for c in range(0, h_ref.shape[1], MXU_COLS)]
    gu = [(_dot(xn, wg_ref[:, cs].astype(jnp.bfloat16)), _dot(xn, wu_ref[:, cs].astype(jnp.bfloat16)))
          for cs in cols]
    for cs, (g, u) in zip(cols, gu):
        h_ref[:, cs] = (g * jax.nn.sigmoid(g) * u).astype(h_ref.dtype)


def _ffn_up(xn, wg, wu):
    T, D = xn.shape
    F = wg.shape[1]
    tm, tf = UP_TM, UP_TF
    assert T % tm == 0 and F % tf == 0 and tf % MXU_COLS == 0
    w_spec = pl.BlockSpec((D, tf), lambda i, j: (0, j))
    return pl.pallas_call(
        _ffn_up_kernel,
        grid=(T // tm, F // tf),
        in_specs=[pl.BlockSpec((tm, D), lambda i, j: (i, 0)), w_spec, w_spec],
        out_specs=pl.BlockSpec((tm, tf), lambda i, j: (i, j)),
        out_shape=jax.ShapeDtypeStruct((T, F), jnp.bfloat16),
        compiler_params=pltpu.CompilerParams(
            dimension_semantics=("parallel", "arbitrary"), vmem_limit_bytes=VMEM_LIMIT),
        name="ffn_up",
    )(xn, wg, wu)


def _ffn_down_kernel(emit_next, h_ref, wd_ref, x_ref, post_g_ref, *refs):
    y = x_ref[...] + FFN_RES_SCALE * _rms(_dot(h_ref[...], wd_ref[...]), post_g_ref[...])
    if emit_next:
        next_g_ref, o_ref, on_ref = refs
        on_ref[...] = _rms(y, next_g_ref[...]).astype(on_ref.dtype)
    else:
        o_ref, = refs
    o_ref[...] = y


def _ffn_down(h, wd, x, post_g, next_g):
    T, F = h.shape
    D = wd.shape[1]
    tm = DOWN_TM
    assert T % tm == 0
    emit_next = next_g is not None
    row_spec = pl.BlockSpec((tm, D), lambda i: (i, 0))
    gain_spec = pl.BlockSpec((1, D), lambda i: (0, 0))
    in_specs = [pl.BlockSpec((tm, F), lambda i: (i, 0)),
                pl.BlockSpec((F, D), lambda i: (0, 0), pipeline_mode=pl.Buffered(1)),
                row_spec, gain_spec]
    args = [h, wd, x, post_g]
    out_specs, out_shape = [row_spec], [jax.ShapeDtypeStruct((T, D), jnp.float32)]
    if emit_next:
        in_specs.append(gain_spec)
        args.append(next_g)
        out_specs.append(row_spec)
        out_shape.append(jax.ShapeDtypeStruct((T, D), jnp.bfloat16))
    outs = pl.pallas_call(
        functools.partial(_ffn_down_kernel, emit_next),
        grid=(T // tm,),
        in_specs=in_specs,
        out_specs=out_specs,
        out_shape=out_shape,
        compiler_params=pltpu.CompilerParams(
            dimension_semantics=("parallel",), vmem_limit_bytes=VMEM_LIMIT),
        name="ffn_down",
    )(*args)
    return (outs[0], outs[1]) if emit_next else (outs[0], None)


def _proj_kernel(n_rope_tiles, xn_ref, w_ref, cos_ref, sa_ref, sb_ref, o_ref):
    j = pl.program_id(1)

    def tile(rope):
        xn = xn_ref[...]
        ys = [_dot(xn, w_ref[:, c:c + MXU_COLS]) for c in range(0, w_ref.shape[1], MXU_COLS)]
        for n, y in enumerate(ys):
            for h in range(0, MXU_COLS, HEAD_DIM):
                yh = y[:, h:h + HEAD_DIM]
                if rope:
                    yh = (yh * cos_ref[...] + pltpu.roll(yh, HEAD_DIM - ROT_DIM // 2, 1) * sa_ref[...]
                          + pltpu.roll(yh, ROT_DIM // 2, 1) * sb_ref[...])
                o_ref[:, n * MXU_COLS + h:n * MXU_COLS + h + HEAD_DIM] = yh.astype(o_ref.dtype)

    pl.when(j < n_rope_tiles)(functools.partial(tile, True))
    pl.when(j >= n_rope_tiles)(functools.partial(tile, False))


def _proj(xn, w, tables, seq, n_rope_cols):
    T, D = xn.shape
    N = w.shape[1]
    tm, tn = PROJ_TM, PROJ_TN
    assert T % tm == 0 and N % tn == 0 and seq % tm == 0 and n_rope_cols % tn == 0 and tn % MXU_COLS == 0
    pos_tiles = seq // tm
    tab_spec = pl.BlockSpec((tm, HEAD_DIM), lambda i, j: (i % pos_tiles, 0))
    return pl.pallas_call(
        functools.partial(_proj_kernel, n_rope_cols // tn),
        grid=(T // tm, N // tn),
        in_specs=[
            pl.BlockSpec((tm, D), lambda i, j: (i, 0)),
            pl.BlockSpec((D, tn), lambda i, j: (0, j)),
            tab_spec, tab_spec, tab_spec,
        ],
        out_specs=pl.BlockSpec((tm, tn), lambda i, j: (i, j)),
        out_shape=jax.ShapeDtypeStruct((T, N), jnp.bfloat16),
        compiler_params=pltpu.CompilerParams(
            dimension_semantics=("parallel", "arbitrary"), vmem_limit_bytes=VMEM_LIMIT),
        name="proj",
    )(xn, w, *tables)


def _rope_tables(seq):
    half = ROT_DIM // 2
    inv_freq = ROPE_THETA ** (-jnp.arange(0, ROT_DIM, 2, dtype=jnp.float32) / ROT_DIM)
    ang = jnp.arange(seq, dtype=jnp.float32)[:, None] * inv_freq[None, :]
    cos, sin = jnp.cos(ang), jnp.sin(ang)
    zeros = jnp.zeros((seq, HEAD_DIM - half), jnp.float32)
    cos_t = jnp.concatenate([cos, cos, jnp.ones((seq, HEAD_DIM - ROT_DIM), jnp.float32)], axis=1)
    sa_t = jnp.concatenate([-sin, zeros], axis=1)
    sb_t = jnp.concatenate([jnp.zeros((seq, half), jnp.float32), sin,
                            jnp.zeros((seq, HEAD_DIM - ROT_DIM), jnp.float32)], axis=1)
    return cos_t, sa_t, sb_t


def _moba_kernel(n_group, q_ref, k_ref, v_ref, o_ref, kaug_ref, vt_ref, kmh_ref, kml_ref):
    qi = pl.program_id(2)
    blk = MOBA_BLOCK
    S = k_ref.shape[1]
    nb = S // blk
    exp2_scale = HEAD_DIM ** -0.5 * LOG2_E
    heads = [slice(g * HEAD_DIM, (g + 1) * HEAD_DIM) for g in range(n_group)]

    @pl.when(qi == 0)
    def _():
        key_block = lax.shift_right_logical(lax.broadcasted_iota(jnp.int32, (S, LANES), 0),
                                            blk.bit_length() - 1)
        one_hot = (key_block == lax.broadcasted_iota(jnp.int32, (S, LANES), 1)).astype(jnp.bfloat16)
        kmh_ref[...] = jnp.zeros_like(kmh_ref)
        kml_ref[...] = jnp.zeros_like(kml_ref)
        for g, hs in enumerate(heads):
            kaug_ref[g, :, :HEAD_DIM] = k_ref[0, :, hs]
            kaug_ref[g, :, HEAD_DIM:] = one_hot
            for c in range(0, S, blk):
                vt_ref[g, :, c:c + blk] = v_ref[0, c:c + blk, hs].astype(jnp.float32).T.astype(vt_ref.dtype)
            for n in range(nb):
                km = jnp.mean(k_ref[0, n * blk:(n + 1) * blk, hs].astype(jnp.float32), axis=0, keepdims=True)
                hi = km.astype(jnp.bfloat16)
                kmh_ref[g, n:n + 1, :] = hi
                kml_ref[g, n:n + 1, :] = (km - hi.astype(jnp.float32)).astype(jnp.bfloat16)

    lane = lax.broadcasted_iota(jnp.int32, (blk, LANES), 1)
    lane_f = lane.astype(jnp.float32)
    past = lane < qi
    q_augs = []
    for g, hs in enumerate(heads):
        q = q_ref[0, :, hs]
        gate = _dot_nt(q, kmh_ref[g]) + _dot_nt(q, kml_ref[g])
        gate = jnp.where(past, gate, -jnp.inf)
        attend = lane == qi
        for _ in range(MOBA_TOPK):
            m = jnp.max(gate, axis=1, keepdims=True)
            first = jnp.min(jnp.where(gate == m, lane_f, float(LANES)), axis=1, keepdims=True)
            hit = lane_f == first
            attend = attend | (hit & past)
            gate = jnp.where(hit, -jnp.inf, gate)
        penalty = jnp.where(attend, 0.0, NEG_BIG).astype(jnp.bfloat16)
        q_augs.append(jnp.concatenate([q, penalty], axis=1))

    def scores(kj):
        start = pl.multiple_of(kj * blk, blk)
        return start, [_dot_nt(kaug_ref[g, pl.ds(start, blk), :], q_augs[g]) for g in range(n_group)]

    key = lax.broadcasted_iota(jnp.int32, (blk, blk), 0)
    query = lax.broadcasted_iota(jnp.int32, (blk, blk), 1)
    start, ss = scores(qi)
    carry = []
    for g in range(n_group):
        s = jnp.where(key <= query, ss[g], NEG_BIG)
        m0 = jnp.max(s, axis=0, keepdims=True)
        p = jnp.exp2((s - m0) * exp2_scale)
        l0 = jnp.sum(p, axis=0, keepdims=True)
        carry.append((m0, l0, _dot(vt_ref[g, :, pl.ds(start, blk)], p.astype(jnp.bfloat16))))

    def body(kj, carry):
        start, ss = scores(kj)
        out = []
        for g in range(n_group):
            m_prev, l_prev, acc_prev = carry[g]
            m_new = jnp.maximum(m_prev, jnp.max(ss[g], axis=0, keepdims=True))
            alpha = jnp.exp2((m_prev - m_new) * exp2_scale)
            p = jnp.exp2((ss[g] - m_new) * exp2_scale)
            l_new = alpha * l_prev + jnp.sum(p, axis=0, keepdims=True)
            acc_new = alpha * acc_prev + _dot(vt_ref[g, :, pl.ds(start, blk)], p.astype(jnp.bfloat16))
            out.append((m_new, l_new, acc_new))
        return tuple(out)

    carry = lax.fori_loop(0, qi, body, tuple(carry))
    for g, hs in enumerate(heads):
        _, l_fin, acc = carry[g]
        o_ref[0, :, hs] = (acc / l_fin).T.astype(o_ref.dtype)


def _moba(qkv, n_heads, q_col, k_col, v_col):
    B, S, _ = qkv.shape
    blk, G = MOBA_BLOCK, MOBA_GROUP
    assert S % blk == 0 and S // blk <= LANES and blk & (blk - 1) == 0
    assert n_heads % G == 0 and q_col % G == 0 and k_col % G == 0 and v_col % G == 0
    W = G * HEAD_DIM
    return pl.pallas_call(
        functools.partial(_moba_kernel, G),
        grid=(B, n_heads // G, S // blk),
        in_specs=[
            pl.BlockSpec((1, blk, W), lambda b, h, i: (b, i, q_col // G + h)),
            pl.BlockSpec((1, S, W), lambda b, h, i: (b, 0, k_col // G + h)),
            pl.BlockSpec((1, S, W), lambda b, h, i: (b, 0, v_col // G + h)),
        ],
        out_specs=pl.BlockSpec((1, blk, W), lambda b, h, i: (b, i, h)),
        out_shape=jax.ShapeDtypeStruct((B, S, n_heads * HEAD_DIM), jnp.bfloat16),
        scratch_shapes=[pltpu.VMEM((G, S, HEAD_DIM + LANES), jnp.bfloat16),
                        pltpu.VMEM((G, HEAD_DIM, S), jnp.bfloat16),
                        pltpu.VMEM((G, LANES, HEAD_DIM), jnp.bfloat16),
                        pltpu.VMEM((G, LANES, HEAD_DIM), jnp.bfloat16)],
        compiler_params=pltpu.CompilerParams(
            dimension_semantics=("parallel", "parallel", "arbitrary")),
        name="moba",
    )(qkv, qkv, qkv)


def _sb_kernel(n_group, q_ref, k_ref, v_ref, tri_ref, o_ref, vt_ref):
    qi = pl.program_id(2)
    blk = SB_BLOCK
    S = k_ref.shape[1]
    scale = HEAD_DIM ** -0.5
    neg_tri = tri_ref[...]
    heads = [slice(g * HEAD_DIM, (g + 1) * HEAD_DIM) for g in range(n_group)]
    qs = [q_ref[0, :, hs] for hs in heads]

    @pl.when(qi == 0)
    def _():
        for g, hs in enumerate(heads):
            for c in range(0, S, blk):
                vt_ref[g, :, c:c + blk] = v_ref[0, c:c + blk, hs].astype(jnp.float32).T.astype(vt_ref.dtype)

    def block(kj, carry, diagonal):
        start = pl.multiple_of(kj * blk, blk)
        if diagonal:
            key = lax.broadcasted_iota(jnp.int32, (blk, blk), 0)
            query = lax.broadcasted_iota(jnp.int32, (blk, blk), 1)
            causal = key < query
        zs = [_dot_nt(k_ref[0, pl.ds(start, blk), hs], qs[g]) * scale for g, hs in enumerate(heads)]
        log_betas, col_sums, sticks = [], [], []
        for g in range(n_group):
            z = zs[g]
            softplus = jnp.maximum(z, 0.0) + jnp.log(1.0 + jnp.exp2(jnp.abs(z) * -LOG2_E))
            log_betas.append(z - softplus)
            if diagonal:
                softplus = jnp.where(causal, softplus, 0.0)
            col_sums.append(jnp.sum(softplus, axis=0, keepdims=True))
            sticks.append(_dot(neg_tri, softplus.astype(jnp.bfloat16)))
        out = []
        for g in range(n_group):
            rest, acc = carry[g]
            a = jnp.exp(log_betas[g] + sticks[g] + rest)
            if diagonal:
                a = jnp.where(causal, a, 0.0)
            acc = acc + _dot(vt_ref[g, :, pl.ds(start, blk)], a.astype(jnp.bfloat16))
            out.append((rest - col_sums[g], acc))
        return tuple(out)

    init = tuple((jnp.zeros((1, blk), jnp.float32), jnp.zeros((HEAD_DIM, blk), jnp.float32))
                 for _ in heads)
    carry = block(qi, init, True)
    carry = lax.fori_loop(0, qi, lambda step, c: block(qi - 1 - step, c, False), carry)
    for g, hs in enumerate(heads):
        o_ref[0, :, hs] = carry[g][1].T.astype(o_ref.dtype)


def _sb(qkv, n_heads, q_col, k_col, v_col):
    B, S, _ = qkv.shape
    blk, G = SB_BLOCK, SB_GROUP
    assert S % blk == 0 and n_heads % G == 0 and q_col % G == 0 and k_col % G == 0 and v_col % G == 0
    idx = jnp.arange(blk)
    neg_tri = -(idx[None, :] > idx[:, None]).astype(jnp.bfloat16)
    W = G * HEAD_DIM
    return pl.pallas_call(
        functools.partial(_sb_kernel, G),
        grid=(B, n_heads // G, S // blk),
        in_specs=[
            pl.BlockSpec((1, blk, W), lambda b, h, i: (b, i, q_col // G + h)),
            pl.BlockSpec((1, S, W), lambda b, h, i: (b, 0, k_col // G + h)),
            pl.BlockSpec((1, S, W), lambda b, h, i: (b, 0, v_col // G + h)),
            pl.BlockSpec((blk, blk), lambda b, h, i: (0, 0)),
        ],
        out_specs=pl.BlockSpec((1, blk, W), lambda b, h, i: (b, i, h)),
        out_shape=jax.ShapeDtypeStruct((B, S, n_heads * HEAD_DIM), jnp.bfloat16),
        scratch_shapes=[pltpu.VMEM((G, HEAD_DIM, S), jnp.bfloat16)],
        compiler_params=pltpu.CompilerParams(
            dimension_semantics=("parallel", "parallel", "arbitrary")),
        name="sb",
    )(qkv, qkv, qkv, neg_tri)


def _out_kernel(x_ref, oa_ref, ob_ref, ga_ref, gb_ref, wa_ref, wb_ref, g_ref, next_g_ref, o_ref, on_ref):
    na = _rms(oa_ref[...].astype(jnp.float32), ga_ref[...]).astype(jnp.bfloat16)
    nb = _rms(ob_ref[...].astype(jnp.float32), gb_ref[...]).astype(jnp.bfloat16)
    y = x_ref[...] + _rms(_dot(na, wa_ref[...]) + _dot(nb, wb_ref[...]), g_ref[...])
    o_ref[...] = y
    on_ref[...] = _rms(y, next_g_ref[...]).astype(on_ref.dtype)


def _out(x, oa, ob, ga, gb, w, g, next_g):
    T, D = x.shape
    Wa, Wb = oa.shape[1], ob.shape[1]
    assert Wa == Wb and w.shape[0] == Wa + Wb
    tm = OUT_TM
    assert T % tm == 0
    return pl.pallas_call(
        _out_kernel,
        grid=(T // tm,),
        in_specs=[
            pl.BlockSpec((tm, D), lambda i: (i, 0)),
            pl.BlockSpec((tm, Wa), lambda i: (i, 0)),
            pl.BlockSpec((tm, Wb), lambda i: (i, 0)),
            pl.BlockSpec((1, Wa), lambda i: (0, 0)),
            pl.BlockSpec((1, Wb), lambda i: (0, 0)),
            pl.BlockSpec((Wa, D), lambda i: (0, 0)),
            pl.BlockSpec((Wb, D), lambda i: (1, 0)),
            pl.BlockSpec((1, D), lambda i: (0, 0)),
            pl.BlockSpec((1, D), lambda i: (0, 0)),
        ],
        out_specs=[pl.BlockSpec((tm, D), lambda i: (i, 0)), pl.BlockSpec((tm, D), lambda i: (i, 0))],
        out_shape=[jax.ShapeDtypeStruct((T, D), jnp.float32), jax.ShapeDtypeStruct((T, D), jnp.bfloat16)],
        compiler_params=pltpu.CompilerParams(
            dimension_semantics=("parallel",), vmem_limit_bytes=VMEM_LIMIT),
        name="out_proj",
    )(x, oa, ob, ga, gb, w, w, g, next_g)


def kernel(x, ffn1_pre_g, ffn1_w_gate, ffn1_w_up, ffn1_w_down, ffn1_post_g, mix_pre_g, w_in, moba_out_g, sb_out_g, w_out, mix_post_g, ffn2_pre_g, ffn2_w_gate, ffn2_w_up, ffn2_w_down, ffn2_post_g):
    B, S, D = x.shape
    depth = w_in.shape[0]
    w_moba = moba_out_g.shape[1]
    w_sb = sb_out_g.shape[1]
    h_moba, h_sb = w_moba // HEAD_DIM, w_sb // HEAD_DIM
    bf = lambda w: w.astype(jnp.bfloat16)
    tables = _rope_tables(S)

    xt = x.reshape(B * S, D)
    for l in range(depth):
        h = _ffn_up(_prenorm(xt, ffn1_pre_g[l:l + 1]), ffn1_w_gate[l], ffn1_w_up[l])
        xt, xn = _ffn_down(h, bf(ffn1_w_down[l]), xt, ffn1_post_g[l:l + 1], mix_pre_g[l:l + 1])
        qkv = _proj(xn, bf(w_in[l]), tables, S, 2 * w_moba).reshape(B, S, -1)
        o_a = _moba(qkv, h_moba, 0, h_moba, 2 * h_moba)
        o_b = _sb(qkv, h_sb, 3 * h_moba, 3 * h_moba + h_sb, 3 * h_moba + 2 * h_sb)
        xt, xn = _out(xt, o_a.reshape(B * S, w_moba), o_b.reshape(B * S, w_sb),
                      moba_out_g[l:l + 1], sb_out_g[l:l + 1], bf(w_out[l]), mix_post_g[l:l + 1],
                      ffn2_pre_g[l:l + 1])
        h = _ffn_up(xn, ffn2_w_gate[l], ffn2_w_up[l])
        xt, _ = _ffn_down(h, bf(ffn2_w_down[l]), xt, ffn2_post_g[l:l + 1], None)
    return xt.reshape(B, S, D)
```

```python
import functools

import jax
import jax.numpy as jnp
from jax import lax
from jax.experimental import pallas as pl
from jax.experimental.pallas import tpu as pltpu

HEAD_DIM = 128
MOBA_BLOCK = 256
MOBA_TOPK = 3
ROPE_THETA = 500000.0
ROT_DIM = HEAD_DIM // 4
RMS_EPS = 1e-6
FFN_RES_SCALE = 0.5

LANES = 128
BF16_ROWS = 16
LOG2_E = 1.4426950408889634
NEG_BIG = -1e30
MOBA_GROUP = 8
SB_BLOCK = 256
SB_GROUP = 8
VMEM_LIMIT = 56 * 1024 * 1024

NORM_TM = 1024
UP_TM, UP_TF = 2048, 512
DOWN_TM = 256
PROJ_TM, PROJ_TN = 1024, 1024
MXU_COLS = 256
OUT_TM = 512


def _rms(xf, g):
    return xf * lax.rsqrt(jnp.mean(xf * xf, axis=-1, keepdims=True) + RMS_EPS) * g


def _dot(a, b):
    return jnp.dot(a, b, preferred_element_type=jnp.float32)


def _dot_nt(a, b):
    return lax.dot_general(a, b, (((1,), (1,)), ((), ())), preferred_element_type=jnp.float32)


def _prenorm_kernel(x_ref, g_ref, o_ref):
    o_ref[...] = _rms(x_ref[...], g_ref[...]).astype(o_ref.dtype)


def _prenorm(x, g):
    T, D = x.shape
    tm = NORM_TM
    assert T % tm == 0
    return pl.pallas_call(
        _prenorm_kernel,
        grid=(T // tm,),
        in_specs=[pl.BlockSpec((tm, D), lambda i: (i, 0)), pl.BlockSpec((1, D), lambda i: (0, 0))],
        out_specs=pl.BlockSpec((tm, D), lambda i: (i, 0)),
        out_shape=jax.ShapeDtypeStruct((T, D), jnp.bfloat16),
        compiler_params=pltpu.CompilerParams(
            dimension_semantics=("parallel",), vmem_limit_bytes=VMEM_LIMIT),
        name="prenorm",
    )(x, g)


def _ffn_up_kernel(xn_ref, wg_ref, wu_ref, h_ref):
    xn = xn_ref[...]
    cols = [slice(c, c + MXU_COLS) for c in range(0, h_ref.shape[1], MXU_COLS)]
    gu = [(_dot(xn, wg_ref[:, cs].astype(jnp.bfloat16)), _dot(xn, wu_ref[:, cs].astype(jnp.bfloat16)))
          for cs in cols]
    for cs, (g, u) in zip(cols, gu):
        h_ref[:, cs] = (g * jax.nn.sigmoid(g) * u).astype(h_ref.dtype)


def _ffn_up(xn, wg, wu):
    T, D = xn.shape
    F = wg.shape[1]
    tm, tf = UP_TM, UP_TF
    assert T % tm == 0 and F % tf == 0 and tf % MXU_COLS == 0
    w_spec = pl.BlockSpec((D, tf), lambda i, j: (0, j))
    return pl.pallas_call(
        _ffn_up_kernel,
        grid=(T // tm, F // tf),
        in_specs=[pl.BlockSpec((tm, D), lambda i, j: (i, 0)), w_spec, w_spec],
        out_specs=pl.BlockSpec((tm, tf), lambda i, j: (i, j)),
        out_shape=jax.ShapeDtypeStruct((T, F), jnp.bfloat16),
        compiler_params=pltpu.CompilerParams(
            dimension_semantics=("parallel", "arbitrary"), vmem_limit_bytes=VMEM_LIMIT),
        name="ffn_up",
    )(xn, wg, wu)


def _ffn_down_kernel(emit_next, h_ref, wd_ref, x_ref, post_g_ref, *refs):
    y = x_ref[...] + FFN_RES_SCALE * _rms(_dot(h_ref[...], wd_ref[...]), post_g_ref[...])
    if emit_next:
        next_g_ref, o_ref, on_ref = refs
        on_ref[...] = _rms(y, next_g_ref[...]).astype(on_ref.dtype)
    else:
        o_ref, = refs
    o_ref[...] = y


def _ffn_down(h, wd, x, post_g, next_g):
    T, F = h.shape
    D = wd.shape[1]
    tm = DOWN_TM
    assert T % tm == 0
    emit_next = next_g is not None
    row_spec = pl.BlockSpec((tm, D), lambda i: (i, 0))
    gain_spec = pl.BlockSpec((1, D), lambda i: (0, 0))
    in_specs = [pl.BlockSpec((tm, F), lambda i: (i, 0)),
                pl.BlockSpec((F, D), lambda i: (0, 0), pipeline_mode=pl.Buffered(1)),
                row_spec, gain_spec]
    args = [h, wd, x, post_g]
    out_specs, out_shape = [row_spec], [jax.ShapeDtypeStruct((T, D), jnp.float32)]
    if emit_next:
        in_specs.append(gain_spec)
        args.append(next_g)
        out_specs.append(row_spec)
        out_shape.append(jax.ShapeDtypeStruct((T, D), jnp.bfloat16))
    outs = pl.pallas_call(
        functools.partial(_ffn_down_kernel, emit_next),
        grid=(T // tm,),
        in_specs=in_specs,
        out_specs=out_specs,
        out_shape=out_shape,
        compiler_params=pltpu.CompilerParams(
            dimension_semantics=("parallel",), vmem_limit_bytes=VMEM_LIMIT),
        name="ffn_down",
    )(*args)
    return (outs[0], outs[1]) if emit_next else (outs[0], None)


def _proj_kernel(n_rope_tiles, xn_ref, w_ref, cos_ref, sa_ref, sb_ref, o_ref):
    j = pl.program_id(1)

    def tile(rope):
        xn = xn_ref[...]
        ys = [_dot(xn, w_ref[:, c:c + MXU_COLS]) for c in range(0, w_ref.shape[1], MXU_COLS)]
        for n, y in enumerate(ys):
            for h in range(0, MXU_COLS, HEAD_DIM):
                yh = y[:, h:h + HEAD_DIM]
                if rope:
                    yh = (yh * cos_ref[...] + pltpu.roll(yh, HEAD_DIM - ROT_DIM // 2, 1) * sa_ref[...]
                          + pltpu.roll(yh, ROT_DIM // 2, 1) * sb_ref[...])
                o_ref[:, n * MXU_COLS + h:n * MXU_COLS + h + HEAD_DIM] = yh.astype(o_ref.dtype)

    pl.when(j < n_rope_tiles)(functools.partial(tile, True))
    pl.when(j >= n_rope_tiles)(functools.partial(tile, False))


def _proj(xn, w, tables, seq, n_rope_cols):
    T, D = xn.shape
    N = w.shape[1]
    tm, tn = PROJ_TM, PROJ_TN
    assert T % tm == 0 and N % tn == 0 and seq % tm == 0 and n_rope_cols % tn == 0 and tn % MXU_COLS == 0
    pos_tiles = seq // tm
    tab_spec = pl.BlockSpec((tm, HEAD_DIM), lambda i, j: (i % pos_tiles, 0))
    return pl.pallas_call(
        functools.partial(_proj_kernel, n_rope_cols // tn),
        grid=(T // tm, N // tn),
        in_specs=[
            pl.BlockSpec((tm, D), lambda i, j: (i, 0)),
            pl.BlockSpec((D, tn), lambda i, j: (0, j)),
            tab_spec, tab_spec, tab_spec,
        ],
        out_specs=pl.BlockSpec((tm, tn), lambda i, j: (i, j)),
        out_shape=jax.ShapeDtypeStruct((T, N), jnp.bfloat16),
        compiler_params=pltpu.CompilerParams(
            dimension_semantics=("parallel", "arbitrary"), vmem_limit_bytes=VMEM_LIMIT),
        name="proj",
    )(xn, w, *tables)


def _rope_tables(seq):
    half = ROT_DIM // 2
    inv_freq = ROPE_THETA ** (-jnp.arange(0, ROT_DIM, 2, dtype=jnp.float32) / ROT_DIM)
    ang = jnp.arange(seq, dtype=jnp.float32)[:, None] * inv_freq[None, :]
    cos, sin = jnp.cos(ang), jnp.sin(ang)
    zeros = jnp.zeros((seq, HEAD_DIM - half), jnp.float32)
    cos_t = jnp.concatenate([cos, cos, jnp.ones((seq, HEAD_DIM - ROT_DIM), jnp.float32)], axis=1)
    sa_t = jnp.concatenate([-sin, zeros], axis=1)
    sb_t = jnp.concatenate([jnp.zeros((seq, half), jnp.float32), sin,
                            jnp.zeros((seq, HEAD_DIM - ROT_DIM), jnp.float32)], axis=1)
    return cos_t, sa_t, sb_t


def _moba_kernel(n_group, q_ref, k_ref, v_ref, o_ref, kaug_ref, vt_ref, kmh_ref, kml_ref):
    qi = pl.program_id(2)
    blk = MOBA_BLOCK
    S = k_ref.shape[1]
    nb = S // blk
    nbp = -(-nb // BF16_ROWS) * BF16_ROWS
    exp2_scale = HEAD_DIM ** -0.5 * LOG2_E
    heads = [slice(g * HEAD_DIM, (g + 1) * HEAD_DIM) for g in range(n_group)]

    @pl.when(qi == 0)
    def _():
        key_block = lax.shift_right_logical(lax.broadcasted_iota(jnp.int32, (S, LANES), 0),
                                            blk.bit_length() - 1)
        one_hot = (key_block == lax.broadcasted_iota(jnp.int32, (S, LANES), 1)).astype(jnp.bfloat16)
        kmh_ref[...] = jnp.zeros_like(kmh_ref)
        kml_ref[...] = jnp.zeros_like(kml_ref)
        for g, hs in enumerate(heads):
            kaug_ref[g, :, :HEAD_DIM] = k_ref[0, :, hs]
            kaug_ref[g, :, HEAD_DIM:] = one_hot
            for c in range(0, S, blk):
                vt_ref[g, :, c:c + blk] = v_ref[0, c:c + blk, hs].astype(jnp.float32).T.astype(vt_ref.dtype)
            for n in range(nb):
                km = jnp.mean(k_ref[0, n * blk:(n + 1) * blk, hs].astype(jnp.float32), axis=0, keepdims=True)
                hi = km.astype(jnp.bfloat16)
                kmh_ref[g, n:n + 1, :] = hi
                kml_ref[g, n:n + 1, :] = (km - hi.astype(jnp.float32)).astype(jnp.bfloat16)

    block_id = lax.broadcasted_iota(jnp.int32, (nbp, blk), 0)
    block_f = block_id.astype(jnp.float32)
    past = block_id < qi
    q_augs = []
    for g, hs in enumerate(heads):
        qt = q_ref[0, :, hs].astype(jnp.float32).T.astype(jnp.bfloat16)
        gate = _dot(kmh_ref[g, :nbp, :], qt) + _dot(kml_ref[g, :nbp, :], qt)
        gate = jnp.where(past, gate, -jnp.inf)
        attend = block_id == qi
        for _ in range(MOBA_TOPK):
            m = jnp.max(gate, axis=0, keepdims=True)
            first = jnp.min(jnp.where(gate == m, block_f, float(nbp)), axis=0, keepdims=True)
            hit = block_f == first
            attend = attend | (hit & past)
            gate = jnp.where(hit, -jnp.inf, gate)
        penalty = jnp.where(attend, 0.0, NEG_BIG).astype(jnp.bfloat16)
        q_augs.append(jnp.concatenate(
            [qt, penalty, jnp.zeros((LANES - nbp, blk), jnp.bfloat16)], axis=0))

    def scores(kj):
        start = pl.multiple_of(kj * blk, blk)
        return start, [_dot(kaug_ref[g, pl.ds(start, blk), :], q_augs[g]) for g in range(n_group)]

    key = lax.broadcasted_iota(jnp.int32, (blk, blk), 0)
    query = lax.broadcasted_iota(jnp.int32, (blk, blk), 1)
    start, ss = scores(qi)
    carry = []
    for g in range(n_group):
        s = jnp.where(key <= query, ss[g], NEG_BIG)
        m0 = jnp.max(s, axis=0, keepdims=True)
        p = jnp.exp2((s - m0) * exp2_scale)
        l0 = jnp.sum(p, axis=0, keepdims=True)
        carry.append((m0, l0, _dot(vt_ref[g, :, pl.ds(start, blk)], p.astype(jnp.bfloat16))))

    def body(kj, carry):
        start, ss = scores(kj)
        out = []
        for g in range(n_group):
            m_prev, l_prev, acc_prev = carry[g]
            m_new = jnp.maximum(m_prev, jnp.max(ss[g], axis=0, keepdims=True))
            alpha = jnp.exp2((m_prev - m_new) * exp2_scale)
            p = jnp.exp2((ss[g] - m_new) * exp2_scale)
            l_new = alpha * l_prev + jnp.sum(p, axis=0, keepdims=True)
            acc_new = alpha * acc_prev + _dot(vt_ref[g, :, pl.ds(start, blk)], p.astype(jnp.bfloat16))
            out.append((m_new, l_new, acc_new))
        return tuple(out)

    carry = lax.fori_loop(0, qi, body, tuple(carry))
    for g, hs in enumerate(heads):
        _, l_fin, acc = carry[g]
        o_ref[0, :, hs] = (acc / l_fin).T.astype(o_ref.dtype)


def _moba(qkv, n_heads, q_col, k_col, v_col):
    B, S, _ = qkv.shape
    blk, G = MOBA_BLOCK, MOBA_GROUP
    assert S % blk == 0 and S // blk <= LANES and blk & (blk - 1) == 0
    assert n_heads % G == 0 and q_col % G == 0 and k_col % G == 0 and v_col % G == 0
    W = G * HEAD_DIM
    return pl.pallas_call(
        functools.partial(_moba_kernel, G),
        grid=(B, n_heads // G, S // blk),
        in_specs=[
            pl.BlockSpec((1, blk, W), lambda b, h, i: (b, i, q_col // G + h)),
            pl.BlockSpec((1, S, W), lambda b, h, i: (b, 0, k_col // G + h)),
            pl.BlockSpec((1, S, W), lambda b, h, i: (b, 0, v_col // G + h)),
        ],
        out_specs=pl.BlockSpec((1, blk, W), lambda b, h, i: (b, i, h)),
        out_shape=jax.ShapeDtypeStruct((B, S, n_heads * HEAD_DIM), jnp.bfloat16),
        scratch_shapes=[pltpu.VMEM((G, S, HEAD_DIM + LANES), jnp.bfloat16),
                        pltpu.VMEM((G, HEAD_DIM, S), jnp.bfloat16),
                        pltpu.VMEM((G, LANES, HEAD_DIM), jnp.bfloat16),
                        pltpu.VMEM((G, LANES, HEAD_DIM), jnp.bfloat16)],
        compiler_params=pltpu.CompilerParams(
            dimension_semantics=("parallel", "parallel", "arbitrary")),
        name="moba",
    )(qkv, qkv, qkv)


def _sb_kernel(n_group, q_ref, k_ref, v_ref, tri_ref, o_ref, vt_ref):
    qi = pl.program_id(2)
    blk = SB_BLOCK
    S = k_ref.shape[1]
    scale = HEAD_DIM ** -0.5
    neg_tri = tri_ref[...]
    heads = [slice(g * HEAD_DIM, (g + 1) * HEAD_DIM) for g in range(n_group)]
    qts = [q_ref[0, :, hs].astype(jnp.float32).T.astype(jnp.bfloat16) for hs in heads]

    @pl.when(qi == 0)
    def _():
        for g, hs in enumerate(heads):
            for c in range(0, S, blk):
                vt_ref[g, :, c:c + blk] = v_ref[0, c:c + blk, hs].astype(jnp.float32).T.astype(vt_ref.dtype)

    def block(kj, carry, diagonal):
        start = pl.multiple_of(kj * blk, blk)
        if diagonal:
            key = lax.broadcasted_iota(jnp.int32, (blk, blk), 0)
            query = lax.broadcasted_iota(jnp.int32, (blk, blk), 1)
            causal = key < query
        zs = [_dot(k_ref[0, pl.ds(start, blk), hs], qts[g]) * scale for g, hs in enumerate(heads)]
        log_betas, col_sums, sticks = [], [], []
        for g in range(n_group):
            z = zs[g]
            softplus = jnp.maximum(z, 0.0) + jnp.log(1.0 + jnp.exp2(jnp.abs(z) * -LOG2_E))
            log_betas.append(z - softplus)
            if diagonal:
                softplus = jnp.where(causal, softplus, 0.0)
            col_sums.append(jnp.sum(softplus, axis=0, keepdims=True))
            sticks.append(_dot(neg_tri, softplus.astype(jnp.bfloat16)))
        out = []
        for g in range(n_group):
            rest, acc = carry[g]
            a = jnp.exp(log_betas[g] + sticks[g] + rest)
            if diagonal:
                a = jnp.where(causal, a, 0.0)
            acc = acc + _dot(vt_ref[g, :, pl.ds(start, blk)], a.astype(jnp.bfloat16))
            out.append((rest - col_sums[g], acc))
        return tuple(out)

    init = tuple((jnp.zeros((1, blk), jnp.float32), jnp.zeros((HEAD_DIM, blk), jnp.float32))
                 for _ in heads)
    carry = block(qi, init, True)
    carry = lax.fori_loop(0, qi, lambda step, c: block(qi - 1 - step, c, False), carry)
    for g, hs in enumerate(heads):
        o_ref[0, :, hs] = carry[g][1].T.astype(o_ref.dtype)


def _sb(qkv, n_heads, q_col, k_col, v_col):
    B, S, _ = qkv.shape
    blk, G = SB_BLOCK, SB_GROUP
    assert S % blk == 0 and n_heads % G == 0 and q_col % G == 0 and k_col % G == 0 and v_col % G == 0
    idx = jnp.arange(blk)
    neg_tri = -(idx[None, :] > idx[:, None]).astype(jnp.bfloat16)
    W = G * HEAD_DIM
    return pl.pallas_call(
        functools.partial(_sb_kernel, G),
        grid=(B, n_heads // G, S // blk),
        in_specs=[
            pl.BlockSpec((1, blk, W), lambda b, h, i: (b, i, q_col // G + h)),
            pl.BlockSpec((1, S, W), lambda b, h, i: (b, 0, k_col // G + h)),
            pl.BlockSpec((1, S, W), lambda b, h, i: (b, 0, v_col // G + h)),
            pl.BlockSpec((blk, blk), lambda b, h, i: (0, 0)),
        ],
        out_specs=pl.BlockSpec((1, blk, W), lambda b, h, i: (b, i, h)),
        out_shape=jax.ShapeDtypeStruct((B, S, n_heads * HEAD_DIM), jnp.bfloat16),
        scratch_shapes=[pltpu.VMEM((G, HEAD_DIM, S), jnp.bfloat16)],
        compiler_params=pltpu.CompilerParams(
            dimension_semantics=("parallel", "parallel", "arbitrary")),
        name="sb",
    )(qkv, qkv, qkv, neg_tri)


def _out_kernel(x_ref, oa_ref, ob_ref, ga_ref, gb_ref, wa_ref, wb_ref, g_ref, next_g_ref, o_ref, on_ref):
    na = _rms(oa_ref[...].astype(jnp.float32), ga_ref[...]).astype(jnp.bfloat16)
    nb = _rms(ob_ref[...].astype(jnp.float32), gb_ref[...]).astype(jnp.bfloat16)
    y = x_ref[...] + _rms(_dot(na, wa_ref[...]) + _dot(nb, wb_ref[...]), g_ref[...])
    o_ref[...] = y
    on_ref[...] = _rms(y, next_g_ref[...]).astype(on_ref.dtype)


def _out(x, oa, ob, ga, gb, w, g, next_g):
    T, D = x.shape
    Wa, Wb = oa.shape[1], ob.shape[1]
    assert Wa == Wb and w.shape[0] == Wa + Wb
    tm = OUT_TM
    assert T % tm == 0
    return pl.pallas_call(
        _out_kernel,
        grid=(T // tm,),
        in_specs=[
            pl.BlockSpec((tm, D), lambda i: (i, 0)),
            pl.BlockSpec((tm, Wa), lambda i: (i, 0)),
            pl.BlockSpec((tm, Wb), lambda i: (i, 0)),
            pl.BlockSpec((1, Wa), lambda i: (0, 0)),
            pl.BlockSpec((1, Wb), lambda i: (0, 0)),
            pl.BlockSpec((Wa, D), lambda i: (0, 0)),
            pl.BlockSpec((Wb, D), lambda i: (1, 0)),
            pl.BlockSpec((1, D), lambda i: (0, 0)),
            pl.BlockSpec((1, D), lambda i: (0, 0)),
        ],
        out_specs=[pl.BlockSpec((tm, D), lambda i: (i, 0)), pl.BlockSpec((tm, D), lambda i: (i, 0))],
        out_shape=[jax.ShapeDtypeStruct((T, D), jnp.float32), jax.ShapeDtypeStruct((T, D), jnp.bfloat16)],
        compiler_params=pltpu.CompilerParams(
            dimension_semantics=("parallel",), vmem_limit_bytes=VMEM_LIMIT),
        name="out_proj",
    )(x, oa, ob, ga, gb, w, w, g, next_g)


def kernel(x, ffn1_pre_g, ffn1_w_gate, ffn1_w_up, ffn1_w_down, ffn1_post_g, mix_pre_g, w_in, moba_out_g, sb_out_g, w_out, mix_post_g, ffn2_pre_g, ffn2_w_gate, ffn2_w_up, ffn2_w_down, ffn2_post_g):
    B, S, D = x.shape
    depth = w_in.shape[0]
    w_moba = moba_out_g.shape[1]
    w_sb = sb_out_g.shape[1]
    h_moba, h_sb = w_moba // HEAD_DIM, w_sb // HEAD_DIM
    bf = lambda w: w.astype(jnp.bfloat16)
    tables = _rope_tables(S)

    xt = x.reshape(B * S, D)
    for l in range(depth):
        h = _ffn_up(_prenorm(xt, ffn1_pre_g[l:l + 1]), ffn1_w_gate[l], ffn1_w_up[l])
        xt, xn = _ffn_down(h, bf(ffn1_w_down[l]), xt, ffn1_post_g[l:l + 1], mix_pre_g[l:l + 1])
        qkv = _proj(xn, bf(w_in[l]), tables, S, 2 * w_moba).reshape(B, S, -1)
        o_a = _moba(qkv, h_moba, 0, h_moba, 2 * h_moba)
        o_b = _sb(qkv, h_sb, 3 * h_moba, 3 * h_moba + h_sb, 3 * h_moba + 2 * h_sb)
        xt, xn = _out(xt, o_a.reshape(B * S, w_moba), o_b.reshape(B * S, w_sb),
                      moba_out_g[l:l + 1], sb_out_g[l:l + 1], bf(w_out[l]), mix_post_g[l:l + 1],
                      ffn2_pre_g[l:l + 1])
        h = _ffn_up(xn, ffn2_w_gate[l], ffn2_w_up[l])
        xt, _ = _ffn_down(h, bf(ffn2_w_down[l]), xt, ffn2_post_g[l:l + 1], None)
    return xt.reshape(B, S, D)
```

```python
import functools

import jax
import jax.numpy as jnp
from jax import lax
from jax.experimental import pallas as pl
from jax.experimental.pallas import tpu as pltpu

HEAD_DIM = 128
MOBA_BLOCK = 256
MOBA_TOPK = 3
ROPE_THETA = 500000.0
ROT_DIM = HEAD_DIM // 4
RMS_EPS = 1e-6
FFN_RES_SCALE = 0.5

LANES = 128
BF16_ROWS = 16
LOG2_E = 1.4426950408889634
NEG_BIG = -1e30
MOBA_GROUP = 8
SB_BLOCK = 256
SB_GROUP = 8
VMEM_LIMIT = 56 * 1024 * 1024

NORM_TM = 1024
UP_TM, UP_TF = 2048, 512
DOWN_TM = 256
PROJ_TM, PROJ_TN = 1024, 1024
MXU_COLS = 256
OUT_TM = 512


def _rms(xf, g):
    return xf * lax.rsqrt(jnp.mean(xf * xf, axis=-1, keepdims=True) + RMS_EPS) * g


def _dot(a, b):
    return jnp.dot(a, b, preferred_element_type=jnp.float32)


def _dot_nt(a, b):
    return lax.dot_general(a, b, (((1,), (1,)), ((), ())), preferred_element_type=jnp.float32)


def _prenorm_kernel(x_ref, g_ref, o_ref):
    o_ref[...] = _rms(x_ref[...], g_ref[...]).astype(o_ref.dtype)


def _prenorm(x, g):
    T, D = x.shape
    tm = NORM_TM
    assert T % tm == 0
    return pl.pallas_call(
        _prenorm_kernel,
        grid=(T // tm,),
        in_specs=[pl.BlockSpec((tm, D), lambda i: (i, 0)), pl.BlockSpec((1, D), lambda i: (0, 0))],
        out_specs=pl.BlockSpec((tm, D), lambda i: (i, 0)),
        out_shape=jax.ShapeDtypeStruct((T, D), jnp.bfloat16),
        compiler_params=pltpu.CompilerParams(
            dimension_semantics=("parallel",), vmem_limit_bytes=VMEM_LIMIT),
        name="prenorm",
    )(x, g)


def _cast_specs(weights, n_steps, step_index):
    specs, shapes = [], []
    for w in weights:
        rows, cols = w.shape
        assert rows % (n_steps * BF16_ROWS) == 0
        specs.append(pl.BlockSpec((rows // n_steps, cols), step_index))
        shapes.append(jax.ShapeDtypeStruct(w.shape, jnp.bfloat16))
    return specs, shapes


def _cast_blocks(src_refs, dst_refs):
    for src, dst in zip(src_refs, dst_refs):
        dst[...] = src[...].astype(dst.dtype)


def _ffn_up_kernel(n_cast, xn_ref, wg_ref, wu_ref, *refs):
    h_ref = refs[n_cast]
    _cast_blocks(refs[:n_cast], refs[n_cast + 1:])
    xn = xn_ref[...]
    cols = [slice(c, c + MXU_COLS) for c in range(0, h_ref.shape[1], MXU_COLS)]
    gu = [(_dot(xn, wg_ref[:, cs].astype(jnp.bfloat16)), _dot(xn, wu_ref[:, cs].astype(jnp.bfloat16)))
          for cs in cols]
    for cs, (g, u) in zip(cols, gu):
        h_ref[:, cs] = (g * jax.nn.sigmoid(g) * u).astype(h_ref.dtype)


def _ffn_up(xn, wg, wu, to_cast):
    T, D = xn.shape
    F = wg.shape[1]
    tm, tf = UP_TM, UP_TF
    assert T % tm == 0 and F % tf == 0 and tf % MXU_COLS == 0
    n_i, n_j = T // tm, F // tf
    w_spec = pl.BlockSpec((D, tf), lambda i, j: (0, j))
    cast_specs, cast_shapes = _cast_specs(to_cast, n_i * n_j, lambda i, j: (i * n_j + j, 0))
    outs = pl.pallas_call(
        functools.partial(_ffn_up_kernel, len(to_cast)),
        grid=(n_i, n_j),
        in_specs=[pl.BlockSpec((tm, D), lambda i, j: (i, 0)), w_spec, w_spec] + cast_specs,
        out_specs=[pl.BlockSpec((tm, tf), lambda i, j: (i, j))] + cast_specs,
        out_shape=[jax.ShapeDtypeStruct((T, F), jnp.bfloat16)] + cast_shapes,
        compiler_params=pltpu.CompilerParams(
            dimension_semantics=("arbitrary", "arbitrary"), vmem_limit_bytes=VMEM_LIMIT),
        name="ffn_up",
    )(xn, wg, wu, *to_cast)
    return outs[0], outs[1:]


def _ffn_down_kernel(emit_next, n_cast, h_ref, wd_ref, x_ref, post_g_ref, *refs):
    n_in = n_cast + emit_next
    n_out = 1 + emit_next
    _cast_blocks(refs[emit_next:n_in], refs[n_in + n_out:])
    o_ref = refs[n_in]
    y = x_ref[...] + FFN_RES_SCALE * _rms(_dot(h_ref[...], wd_ref[...]), post_g_ref[...])
    if emit_next:
        next_g_ref, on_ref = refs[0], refs[n_in + 1]
        on_ref[...] = _rms(y, next_g_ref[...]).astype(on_ref.dtype)
    o_ref[...] = y


def _ffn_down(h, wd, x, post_g, next_g, to_cast):
    T, F = h.shape
    D = wd.shape[1]
    tm = DOWN_TM
    assert T % tm == 0
    emit_next = next_g is not None
    row_spec = pl.BlockSpec((tm, D), lambda i: (i, 0))
    gain_spec = pl.BlockSpec((1, D), lambda i: (0, 0))
    cast_specs, cast_shapes = _cast_specs(to_cast, T // tm, lambda i: (i, 0))
    in_specs = [pl.BlockSpec((tm, F), lambda i: (i, 0)),
                pl.BlockSpec((F, D), lambda i: (0, 0), pipeline_mode=pl.Buffered(1)),
                row_spec, gain_spec]
    args = [h, wd, x, post_g]
    out_specs, out_shape = [row_spec], [jax.ShapeDtypeStruct((T, D), jnp.float32)]
    if emit_next:
        in_specs.append(gain_spec)
        args.append(next_g)
        out_specs.append(row_spec)
        out_shape.append(jax.ShapeDtypeStruct((T, D), jnp.bfloat16))
    outs = pl.pallas_call(
        functools.partial(_ffn_down_kernel, emit_next, len(to_cast)),
        grid=(T // tm,),
        in_specs=in_specs + cast_specs,
        out_specs=out_specs + cast_specs,
        out_shape=out_shape + cast_shapes,
        compiler_params=pltpu.CompilerParams(
            dimension_semantics=("arbitrary",), vmem_limit_bytes=VMEM_LIMIT),
        name="ffn_down",
    )(*args, *to_cast)
    n_out = 1 + emit_next
    return outs[0], (outs[1] if emit_next else None), outs[n_out:]


def _proj_kernel(n_rope_tiles, xn_ref, w_ref, cos_ref, sa_ref, sb_ref, o_ref):
    j = pl.program_id(1)

    def tile(rope):
        xn = xn_ref[...]
        ys = [_dot(xn, w_ref[:, c:c + MXU_COLS]) for c in range(0, w_ref.shape[1], MXU_COLS)]
        for n, y in enumerate(ys):
            for h in range(0, MXU_COLS, HEAD_DIM):
                yh = y[:, h:h + HEAD_DIM]
                if rope:
                    yh = (yh * cos_ref[...] + pltpu.roll(yh, HEAD_DIM - ROT_DIM // 2, 1) * sa_ref[...]
                          + pltpu.roll(yh, ROT_DIM // 2, 1) * sb_ref[...])
                o_ref[:, n * MXU_COLS + h:n * MXU_COLS + h + HEAD_DIM] = yh.astype(o_ref.dtype)

    pl.when(j < n_rope_tiles)(functools.partial(tile, True))
    pl.when(j >= n_rope_tiles)(functools.partial(tile, False))


def _proj(xn, w, tables, seq, n_rope_cols):
    T, D = xn.shape
    N = w.shape[1]
    tm, tn = PROJ_TM, PROJ_TN
    assert T % tm == 0 and N % tn == 0 and seq % tm == 0 and n_rope_cols % tn == 0 and tn % MXU_COLS == 0
    pos_tiles = seq // tm
    tab_spec = pl.BlockSpec((tm, HEAD_DIM), lambda i, j: (i % pos_tiles, 0))
    return pl.pallas_call(
        functools.partial(_proj_kernel, n_rope_cols // tn),
        grid=(T // tm, N // tn),
        in_specs=[
            pl.BlockSpec((tm, D), lambda i, j: (i, 0)),
            pl.BlockSpec((D, tn), lambda i, j: (0, j)),
            tab_spec, tab_spec, tab_spec,
        ],
        out_specs=pl.BlockSpec((tm, tn), lambda i, j: (i, j)),
        out_shape=jax.ShapeDtypeStruct((T, N), jnp.bfloat16),
        compiler_params=pltpu.CompilerParams(
            dimension_semantics=("parallel", "arbitrary"), vmem_limit_bytes=VMEM_LIMIT),
        name="proj",
    )(xn, w, *tables)


def _rope_tables(seq):
    half = ROT_DIM // 2
    inv_freq = ROPE_THETA ** (-jnp.arange(0, ROT_DIM, 2, dtype=jnp.float32) / ROT_DIM)
    ang = jnp.arange(seq, dtype=jnp.float32)[:, None] * inv_freq[None, :]
    cos, sin = jnp.cos(ang), jnp.sin(ang)
    zeros = jnp.zeros((seq, HEAD_DIM - half), jnp.float32)
    cos_t = jnp.concatenate([cos, cos, jnp.ones((seq, HEAD_DIM - ROT_DIM), jnp.float32)], axis=1)
    sa_t = jnp.concatenate([-sin, zeros], axis=1)
    sb_t = jnp.concatenate([jnp.zeros((seq, half), jnp.float32), sin,
                            jnp.zeros((seq, HEAD_DIM - ROT_DIM), jnp.float32)], axis=1)
    return cos_t, sa_t, sb_t


def _moba_kernel(n_group, q_ref, k_ref, v_ref, o_ref, kaug_ref, vt_ref, kmh_ref, kml_ref):
    qi = pl.program_id(2)
    blk = MOBA_BLOCK
    S = k_ref.shape[1]
    nb = S // blk
    nbp = -(-nb // BF16_ROWS) * BF16_ROWS
    exp2_scale = HEAD_DIM ** -0.5 * LOG2_E
    heads = [slice(g * HEAD_DIM, (g + 1) * HEAD_DIM) for g in range(n_group)]

    @pl.when(qi == 0)
    def _():
        key_block = lax.shift_right_logical(lax.broadcasted_iota(jnp.int32, (S, LANES), 0),
                                            blk.bit_length() - 1)
        one_hot = (key_block == lax.broadcasted_iota(jnp.int32, (S, LANES), 1)).astype(jnp.bfloat16)
        kmh_ref[...] = jnp.zeros_like(kmh_ref)
        kml_ref[...] = jnp.zeros_like(kml_ref)
        for g, hs in enumerate(heads):
            kaug_ref[g, :, :HEAD_DIM] = k_ref[0, :, hs]
            kaug_ref[g, :, HEAD_DIM:] = one_hot
            for c in range(0, S, blk):
                vt_ref[g, :, c:c + blk] = v_ref[0, c:c + blk, hs].astype(jnp.float32).T.astype(vt_ref.dtype)
            for n in range(nb):
                km = jnp.mean(k_ref[0, n * blk:(n + 1) * blk, hs].astype(jnp.float32), axis=0, keepdims=True)
                hi = km.astype(jnp.bfloat16)
                kmh_ref[g, n:n + 1, :] = hi
                kml_ref[g, n:n + 1, :] = (km - hi.astype(jnp.float32)).astype(jnp.bfloat16)

    block_id = lax.broadcasted_iota(jnp.int32, (nbp, blk), 0)
    block_f = block_id.astype(jnp.float32)
    past = block_id < qi
    q_augs = []
    for g, hs in enumerate(heads):
        qt = q_ref[0, :, hs].astype(jnp.float32).T.astype(jnp.bfloat16)
        gate = _dot(kmh_ref[g, :nbp, :], qt) + _dot(kml_ref[g, :nbp, :], qt)
        gate = jnp.where(past, gate, -jnp.inf)
        attend = block_id == qi
        for _ in range(MOBA_TOPK):
            m = jnp.max(gate, axis=0, keepdims=True)
            first = jnp.min(jnp.where(gate == m, block_f, float(nbp)), axis=0, keepdims=True)
            hit = block_f == first
            attend = attend | (hit & past)
            gate = jnp.where(hit, -jnp.inf, gate)
        penalty = jnp.where(attend, 0.0, NEG_BIG).astype(jnp.bfloat16)
        q_augs.append(jnp.concatenate(
            [qt, penalty, jnp.zeros((LANES - nbp, blk), jnp.bfloat16)], axis=0))

    def scores(kj):
        start = pl.multiple_of(kj * blk, blk)
        return start, [_dot(kaug_ref[g, pl.ds(start, blk), :], q_augs[g]) for g in range(n_group)]

    key = lax.broadcasted_iota(jnp.int32, (blk, blk), 0)
    query = lax.broadcasted_iota(jnp.int32, (blk, blk), 1)
    start, ss = scores(qi)
    carry = []
    for g in range(n_group):
        s = jnp.where(key <= query, ss[g], NEG_BIG)
        m0 = jnp.max(s, axis=0, keepdims=True)
        p = jnp.exp2((s - m0) * exp2_scale)
        l0 = jnp.sum(p, axis=0, keepdims=True)
        carry.append((m0, l0, _dot(vt_ref[g, :, pl.ds(start, blk)], p.astype(jnp.bfloat16))))

    def body(kj, carry):
        start, ss = scores(kj)
        out = []
        for g in range(n_group):
            m_prev, l_prev, acc_prev = carry[g]
            m_new = jnp.maximum(m_prev, jnp.max(ss[g], axis=0, keepdims=True))
            alpha = jnp.exp2((m_prev - m_new) * exp2_scale)
            p = jnp.exp2((ss[g] - m_new) * exp2_scale)
            l_new = alpha * l_prev + jnp.sum(p, axis=0, keepdims=True)
            acc_new = alpha * acc_prev + _dot(vt_ref[g, :, pl.ds(start, blk)], p.astype(jnp.bfloat16))
            out.append((m_new, l_new, acc_new))
        return tuple(out)

    carry = lax.fori_loop(0, qi, body, tuple(carry))
    for g, hs in enumerate(heads):
        _, l_fin, acc = carry[g]
        o_ref[0, :, hs] = (acc / l_fin).T.astype(o_ref.dtype)


def _moba(qkv, n_heads, q_col, k_col, v_col):
    B, S, _ = qkv.shape
    blk, G = MOBA_BLOCK, MOBA_GROUP
    assert S % blk == 0 and S // blk <= LANES and blk & (blk - 1) == 0
    assert n_heads % G == 0 and q_col % G == 0 and k_col % G == 0 and v_col % G == 0
    W = G * HEAD_DIM
    return pl.pallas_call(
        functools.partial(_moba_kernel, G),
        grid=(B, n_heads // G, S // blk),
        in_specs=[
            pl.BlockSpec((1, blk, W), lambda b, h, i: (b, i, q_col // G + h)),
            pl.BlockSpec((1, S, W), lambda b, h, i: (b, 0, k_col // G + h)),
            pl.BlockSpec((1, S, W), lambda b, h, i: (b, 0, v_col // G + h)),
        ],
        out_specs=pl.BlockSpec((1, blk, W), lambda b, h, i: (b, i, h)),
        out_shape=jax.ShapeDtypeStruct((B, S, n_heads * HEAD_DIM), jnp.bfloat16),
        scratch_shapes=[pltpu.VMEM((G, S, HEAD_DIM + LANES), jnp.bfloat16),
                        pltpu.VMEM((G, HEAD_DIM, S), jnp.bfloat16),
                        pltpu.VMEM((G, LANES, HEAD_DIM), jnp.bfloat16),
                        pltpu.VMEM((G, LANES, HEAD_DIM), jnp.bfloat16)],
        compiler_params=pltpu.CompilerParams(
            dimension_semantics=("parallel", "parallel", "arbitrary")),
        name="moba",
    )(qkv, qkv, qkv)


def _sb_kernel(n_group, q_ref, k_ref, v_ref, tri_ref, o_ref, vt_ref):
    qi = pl.program_id(2)
    blk = SB_BLOCK
    S = k_ref.shape[1]
    scale = HEAD_DIM ** -0.5
    neg_tri = tri_ref[...]
    heads = [slice(g * HEAD_DIM, (g + 1) * HEAD_DIM) for g in range(n_group)]
    qts = [q_ref[0, :, hs].astype(jnp.float32).T.astype(jnp.bfloat16) for hs in heads]

    @pl.when(qi == 0)
    def _():
        for g, hs in enumerate(heads):
            for c in range(0, S, blk):
                vt_ref[g, :, c:c + blk] = v_ref[0, c:c + blk, hs].astype(jnp.float32).T.astype(vt_ref.dtype)

    def block(kj, carry, diagonal):
        start = pl.multiple_of(kj * blk, blk)
        if diagonal:
            key = lax.broadcasted_iota(jnp.int32, (blk, blk), 0)
            query = lax.broadcasted_iota(jnp.int32, (blk, blk), 1)
            causal = key < query
        zs = [_dot(k_ref[0, pl.ds(start, blk), hs], qts[g]) * scale for g, hs in enumerate(heads)]
        log_betas, col_sums, sticks = [], [], []
        for g in range(n_group):
            z = zs[g]
            softplus = jnp.maximum(z, 0.0) + jnp.log(1.0 + jnp.exp2(jnp.abs(z) * -LOG2_E))
            log_betas.append(z - softplus)
            if diagonal:
                softplus = jnp.where(causal, softplus, 0.0)
            col_sums.append(jnp.sum(softplus, axis=0, keepdims=True))
            sticks.append(_dot(neg_tri, softplus.astype(jnp.bfloat16)))
        out = []
        for g in range(n_group):
            rest, acc = carry[g]
            a = jnp.exp(log_betas[g] + sticks[g] + rest)
            if diagonal:
                a = jnp.where(causal, a, 0.0)
            acc = acc + _dot(vt_ref[g, :, pl.ds(start, blk)], a.astype(jnp.bfloat16))
            out.append((rest - col_sums[g], acc))
        return tuple(out)

    init = tuple((jnp.zeros((1, blk), jnp.float32), jnp.zeros((HEAD_DIM, blk), jnp.float32))
                 for _ in heads)
    carry = block(qi, init, True)
    carry = lax.fori_loop(0, qi, lambda step, c: block(qi - 1 - step, c, False), carry)
    for g, hs in enumerate(heads):
        o_ref[0, :, hs] = carry[g][1].T.astype(o_ref.dtype)


def _sb(qkv, n_heads, q_col, k_col, v_col):
    B, S, _ = qkv.shape
    blk, G = SB_BLOCK, SB_GROUP
    assert S % blk == 0 and n_heads % G == 0 and q_col % G == 0 and k_col % G == 0 and v_col % G == 0
    idx = jnp.arange(blk)
    neg_tri = -(idx[None, :] > idx[:, None]).astype(jnp.bfloat16)
    W = G * HEAD_DIM
    return pl.pallas_call(
        functools.partial(_sb_kernel, G),
        grid=(B, n_heads // G, S // blk),
        in_specs=[
            pl.BlockSpec((1, blk, W), lambda b, h, i: (b, i, q_col // G + h)),
            pl.BlockSpec((1, S, W), lambda b, h, i: (b, 0, k_col // G + h)),
            pl.BlockSpec((1, S, W), lambda b, h, i: (b, 0, v_col // G + h)),
            pl.BlockSpec((blk, blk), lambda b, h, i: (0, 0)),
        ],
        out_specs=pl.BlockSpec((1, blk, W), lambda b, h, i: (b, i, h)),
        out_shape=jax.ShapeDtypeStruct((B, S, n_heads * HEAD_DIM), jnp.bfloat16),
        scratch_shapes=[pltpu.VMEM((G, HEAD_DIM, S), jnp.bfloat16)],
        compiler_params=pltpu.CompilerParams(
            dimension_semantics=("parallel", "parallel", "arbitrary")),
        name="sb",
    )(qkv, qkv, qkv, neg_tri)


def _out_kernel(x_ref, oa_ref, ob_ref, ga_ref, gb_ref, wa_ref, wb_ref, g_ref, next_g_ref, o_ref, on_ref):
    na = _rms(oa_ref[...].astype(jnp.float32), ga_ref[...]).astype(jnp.bfloat16)
    nb = _rms(ob_ref[...].astype(jnp.float32), gb_ref[...]).astype(jnp.bfloat16)
    y = x_ref[...] + _rms(_dot(na, wa_ref[...]) + _dot(nb, wb_ref[...]), g_ref[...])
    o_ref[...] = y
    on_ref[...] = _rms(y, next_g_ref[...]).astype(on_ref.dtype)


def _out(x, oa, ob, ga, gb, w, g, next_g):
    T, D = x.shape
    Wa, Wb = oa.shape[1], ob.shape[1]
    assert Wa == Wb and w.shape[0] == Wa + Wb
    tm = OUT_TM
    assert T % tm == 0
    return pl.pallas_call(
        _out_kernel,
        grid=(T // tm,),
        in_specs=[
            pl.BlockSpec((tm, D), lambda i: (i, 0)),
            pl.BlockSpec((tm, Wa), lambda i: (i, 0)),
            pl.BlockSpec((tm, Wb), lambda i: (i, 0)),
            pl.BlockSpec((1, Wa), lambda i: (0, 0)),
            pl.BlockSpec((1, Wb), lambda i: (0, 0)),
            pl.BlockSpec((Wa, D), lambda i: (0, 0)),
            pl.BlockSpec((Wb, D), lambda i: (1, 0)),
            pl.BlockSpec((1, D), lambda i: (0, 0)),
            pl.BlockSpec((1, D), lambda i: (0, 0)),
        ],
        out_specs=[pl.BlockSpec((tm, D), lambda i: (i, 0)), pl.BlockSpec((tm, D), lambda i: (i, 0))],
        out_shape=[jax.ShapeDtypeStruct((T, D), jnp.float32), jax.ShapeDtypeStruct((T, D), jnp.bfloat16)],
        compiler_params=pltpu.CompilerParams(
            dimension_semantics=("parallel",), vmem_limit_bytes=VMEM_LIMIT),
        name="out_proj",
    )(x, oa, ob, ga, gb, w, w, g, next_g)


def kernel(x, ffn1_pre_g, ffn1_w_gate, ffn1_w_up, ffn1_w_down, ffn1_post_g, mix_pre_g, w_in, moba_out_g, sb_out_g, w_out, mix_post_g, ffn2_pre_g, ffn2_w_gate, ffn2_w_up, ffn2_w_down, ffn2_post_g):
    B, S, D = x.shape
    depth = w_in.shape[0]
    w_moba = moba_out_g.shape[1]
    w_sb = sb_out_g.shape[1]
    h_moba, h_sb = w_moba // HEAD_DIM, w_sb // HEAD_DIM
    tables = _rope_tables(S)

    xt = x.reshape(B * S, D)
    for l in range(depth):
        h, (wd1,) = _ffn_up(_prenorm(xt, ffn1_pre_g[l:l + 1]), ffn1_w_gate[l], ffn1_w_up[l],
                            [ffn1_w_down[l]])
        xt, xn, (w_in_bf, w_out_bf) = _ffn_down(h, wd1, xt, ffn1_post_g[l:l + 1], mix_pre_g[l:l + 1],
                                                [w_in[l], w_out[l]])
        qkv = _proj(xn, w_in_bf, tables, S, 2 * w_moba).reshape(B, S, -1)
        o_a = _moba(qkv, h_moba, 0, h_moba, 2 * h_moba)
        o_b = _sb(qkv, h_sb, 3 * h_moba, 3 * h_moba + h_sb, 3 * h_moba + 2 * h_sb)
        xt, xn = _out(xt, o_a.reshape(B * S, w_moba), o_b.reshape(B * S, w_sb),
                      moba_out_g[l:l + 1], sb_out_g[l:l + 1], w_out_bf, mix_post_g[l:l + 1],
                      ffn2_pre_g[l:l + 1])
        h, (wd2,) = _ffn_up(xn, ffn2_w_gate[l], ffn2_w_up[l], [ffn2_w_down[l]])
        xt, _, _ = _ffn_down(h, wd2, xt, ffn2_post_g[l:l + 1], None, [])
    return xt.reshape(B, S, D)
```

```python
import functools

import jax
import jax.numpy as jnp
from jax import lax
from jax.experimental import pallas as pl
from jax.experimental.pallas import tpu as pltpu

HEAD_DIM = 128
MOBA_BLOCK = 256
MOBA_TOPK = 3
ROPE_THETA = 500000.0
ROT_DIM = HEAD_DIM // 4
RMS_EPS = 1e-6
FFN_RES_SCALE = 0.5

LANES = 128
BF16_ROWS = 16
LOG2_E = 1.4426950408889634
NEG_BIG = -1e30
MOBA_GROUP = 8
SB_BLOCK = 256
SB_DEAD_LOG = -104.0
SB_GROUP = 8
VMEM_LIMIT = 56 * 1024 * 1024

NORM_TM = 1024
UP_TM, UP_TF = 2048, 512
DOWN_TM = 256
PROJ_TM, PROJ_TN = 1024, 1024
MXU_COLS = 256
OUT_TM = 512


def _rms(xf, g):
    return xf * lax.rsqrt(jnp.mean(xf * xf, axis=-1, keepdims=True) + RMS_EPS) * g


def _dot(a, b):
    return jnp.dot(a, b, preferred_element_type=jnp.float32)


def _dot_nt(a, b):
    return lax.dot_general(a, b, (((1,), (1,)), ((), ())), preferred_element_type=jnp.float32)


def _prenorm_kernel(x_ref, g_ref, o_ref):
    o_ref[...] = _rms(x_ref[...], g_ref[...]).astype(o_ref.dtype)


def _prenorm(x, g):
    T, D = x.shape
    tm = NORM_TM
    assert T % tm == 0
    return pl.pallas_call(
        _prenorm_kernel,
        grid=(T // tm,),
        in_specs=[pl.BlockSpec((tm, D), lambda i: (i, 0)), pl.BlockSpec((1, D), lambda i: (0, 0))],
        out_specs=pl.BlockSpec((tm, D), lambda i: (i, 0)),
        out_shape=jax.ShapeDtypeStruct((T, D), jnp.bfloat16),
        compiler_params=pltpu.CompilerParams(
            dimension_semantics=("parallel",), vmem_limit_bytes=VMEM_LIMIT),
        name="prenorm",
    )(x, g)


def _cast_specs(weights, n_steps, step_index):
    specs, shapes = [], []
    for w in weights:
        rows, cols = w.shape
        assert rows % (n_steps * BF16_ROWS) == 0
        specs.append(pl.BlockSpec((rows // n_steps, cols), step_index))
        shapes.append(jax.ShapeDtypeStruct(w.shape, jnp.bfloat16))
    return specs, shapes


def _cast_blocks(src_refs, dst_refs):
    for src, dst in zip(src_refs, dst_refs):
        dst[...] = src[...].astype(dst.dtype)


def _ffn_up_kernel(n_cast, xn_ref, wg_ref, wu_ref, *refs):
    h_ref = refs[n_cast]
    _cast_blocks(refs[:n_cast], refs[n_cast + 1:])
    xn = xn_ref[...]
    cols = [slice(c, c + MXU_COLS) for c in range(0, h_ref.shape[1], MXU_COLS)]
    gu = [(_dot(xn, wg_ref[:, cs].astype(jnp.bfloat16)), _dot(xn, wu_ref[:, cs].astype(jnp.bfloat16)))
          for cs in cols]
    for cs, (g, u) in zip(cols, gu):
        h_ref[:, cs] = (g * jax.nn.sigmoid(g) * u).astype(h_ref.dtype)


def _ffn_up(xn, wg, wu, to_cast):
    T, D = xn.shape
    F = wg.shape[1]
    tm, tf = UP_TM, UP_TF
    assert T % tm == 0 and F % tf == 0 and tf % MXU_COLS == 0
    n_i, n_j = T // tm, F // tf
    w_spec = pl.BlockSpec((D, tf), lambda i, j: (0, j))
    cast_specs, cast_shapes = _cast_specs(to_cast, n_i * n_j, lambda i, j: (i * n_j + j, 0))
    outs = pl.pallas_call(
        functools.partial(_ffn_up_kernel, len(to_cast)),
        grid=(n_i, n_j),
        in_specs=[pl.BlockSpec((tm, D), lambda i, j: (i, 0)), w_spec, w_spec] + cast_specs,
        out_specs=[pl.BlockSpec((tm, tf), lambda i, j: (i, j))] + cast_specs,
        out_shape=[jax.ShapeDtypeStruct((T, F), jnp.bfloat16)] + cast_shapes,
        compiler_params=pltpu.CompilerParams(
            dimension_semantics=("arbitrary", "arbitrary"), vmem_limit_bytes=VMEM_LIMIT),
        name="ffn_up",
    )(xn, wg, wu, *to_cast)
    return outs[0], outs[1:]


def _ffn_down_kernel(emit_next, n_cast, h_ref, wd_ref, x_ref, post_g_ref, *refs):
    n_in = n_cast + emit_next
    n_out = 1 + emit_next
    _cast_blocks(refs[emit_next:n_in], refs[n_in + n_out:])
    o_ref = refs[n_in]
    y = x_ref[...] + FFN_RES_SCALE * _rms(_dot(h_ref[...], wd_ref[...]), post_g_ref[...])
    if emit_next:
        next_g_ref, on_ref = refs[0], refs[n_in + 1]
        on_ref[...] = _rms(y, next_g_ref[...]).astype(on_ref.dtype)
    o_ref[...] = y


def _ffn_down(h, wd, x, post_g, next_g, to_cast):
    T, F = h.shape
    D = wd.shape[1]
    tm = DOWN_TM
    assert T % tm == 0
    emit_next = next_g is not None
    row_spec = pl.BlockSpec((tm, D), lambda i: (i, 0))
    gain_spec = pl.BlockSpec((1, D), lambda i: (0, 0))
    cast_specs, cast_shapes = _cast_specs(to_cast, T // tm, lambda i: (i, 0))
    in_specs = [pl.BlockSpec((tm, F), lambda i: (i, 0)),
                pl.BlockSpec((F, D), lambda i: (0, 0), pipeline_mode=pl.Buffered(1)),
                row_spec, gain_spec]
    args = [h, wd, x, post_g]
    out_specs, out_shape = [row_spec], [jax.ShapeDtypeStruct((T, D), jnp.float32)]
    if emit_next:
        in_specs.append(gain_spec)
        args.append(next_g)
        out_specs.append(row_spec)
        out_shape.append(jax.ShapeDtypeStruct((T, D), jnp.bfloat16))
    outs = pl.pallas_call(
        functools.partial(_ffn_down_kernel, emit_next, len(to_cast)),
        grid=(T // tm,),
        in_specs=in_specs + cast_specs,
        out_specs=out_specs + cast_specs,
        out_shape=out_shape + cast_shapes,
        compiler_params=pltpu.CompilerParams(
            dimension_semantics=("arbitrary",), vmem_limit_bytes=VMEM_LIMIT),
        name="ffn_down",
    )(*args, *to_cast)
    n_out = 1 + emit_next
    return outs[0], (outs[1] if emit_next else None), outs[n_out:]


def _proj_kernel(n_rope_tiles, xn_ref, w_ref, cos_ref, sa_ref, sb_ref, o_ref):
    j = pl.program_id(1)

    def tile(rope):
        xn = xn_ref[...]
        ys = [_dot(xn, w_ref[:, c:c + MXU_COLS]) for c in range(0, w_ref.shape[1], MXU_COLS)]
        for n, y in enumerate(ys):
            for h in range(0, MXU_COLS, HEAD_DIM):
                yh = y[:, h:h + HEAD_DIM]
                if rope:
                    yh = (yh * cos_ref[...] + pltpu.roll(yh, HEAD_DIM - ROT_DIM // 2, 1) * sa_ref[...]
                          + pltpu.roll(yh, ROT_DIM // 2, 1) * sb_ref[...])
                o_ref[:, n * MXU_COLS + h:n * MXU_COLS + h + HEAD_DIM] = yh.astype(o_ref.dtype)

    pl.when(j < n_rope_tiles)(functools.partial(tile, True))
    pl.when(j >= n_rope_tiles)(functools.partial(tile, False))


def _proj(xn, w, tables, seq, n_rope_cols):
    T, D = xn.shape
    N = w.shape[1]
    tm, tn = PROJ_TM, PROJ_TN
    assert T % tm == 0 and N % tn == 0 and seq % tm == 0 and n_rope_cols % tn == 0 and tn % MXU_COLS == 0
    pos_tiles = seq // tm
    tab_spec = pl.BlockSpec((tm, HEAD_DIM), lambda i, j: (i % pos_tiles, 0))
    return pl.pallas_call(
        functools.partial(_proj_kernel, n_rope_cols // tn),
        grid=(T // tm, N // tn),
        in_specs=[
            pl.BlockSpec((tm, D), lambda i, j: (i, 0)),
            pl.BlockSpec((D, tn), lambda i, j: (0, j)),
            tab_spec, tab_spec, tab_spec,
        ],
        out_specs=pl.BlockSpec((tm, tn), lambda i, j: (i, j)),
        out_shape=jax.ShapeDtypeStruct((T, N), jnp.bfloat16),
        compiler_params=pltpu.CompilerParams(
            dimension_semantics=("parallel", "arbitrary"), vmem_limit_bytes=VMEM_LIMIT),
        name="proj",
    )(xn, w, *tables)


def _rope_tables(seq):
    half = ROT_DIM // 2
    inv_freq = ROPE_THETA ** (-jnp.arange(0, ROT_DIM, 2, dtype=jnp.float32) / ROT_DIM)
    ang = jnp.arange(seq, dtype=jnp.float32)[:, None] * inv_freq[None, :]
    cos, sin = jnp.cos(ang), jnp.sin(ang)
    zeros = jnp.zeros((seq, HEAD_DIM - half), jnp.float32)
    cos_t = jnp.concatenate([cos, cos, jnp.ones((seq, HEAD_DIM - ROT_DIM), jnp.float32)], axis=1)
    sa_t = jnp.concatenate([-sin, zeros], axis=1)
    sb_t = jnp.concatenate([jnp.zeros((seq, half), jnp.float32), sin,
                            jnp.zeros((seq, HEAD_DIM - ROT_DIM), jnp.float32)], axis=1)
    return cos_t, sa_t, sb_t


def _moba_kernel(n_group, q_ref, k_ref, v_ref, o_ref, kaug_ref, vt_ref, kmh_ref, kml_ref):
    qi = pl.program_id(2)
    blk = MOBA_BLOCK
    S = k_ref.shape[1]
    nb = S // blk
    nbp = -(-nb // BF16_ROWS) * BF16_ROWS
    exp2_scale = HEAD_DIM ** -0.5 * LOG2_E
    heads = [slice(g * HEAD_DIM, (g + 1) * HEAD_DIM) for g in range(n_group)]

    @pl.when(qi == 0)
    def _():
        key_block = lax.shift_right_logical(lax.broadcasted_iota(jnp.int32, (S, LANES), 0),
                                            blk.bit_length() - 1)
        one_hot = (key_block == lax.broadcasted_iota(jnp.int32, (S, LANES), 1)).astype(jnp.bfloat16)
        kmh_ref[...] = jnp.zeros_like(kmh_ref)
        kml_ref[...] = jnp.zeros_like(kml_ref)
        for g, hs in enumerate(heads):
            kaug_ref[g, :, :HEAD_DIM] = k_ref[0, :, hs]
            kaug_ref[g, :, HEAD_DIM:] = one_hot
            for c in range(0, S, blk):
                vt_ref[g, :, c:c + blk] = v_ref[0, c:c + blk, hs].astype(jnp.float32).T.astype(vt_ref.dtype)
            for n in range(nb):
                km = jnp.mean(k_ref[0, n * blk:(n + 1) * blk, hs].astype(jnp.float32), axis=0, keepdims=True)
                hi = km.astype(jnp.bfloat16)
                kmh_ref[g, n:n + 1, :] = hi
                kml_ref[g, n:n + 1, :] = (km - hi.astype(jnp.float32)).astype(jnp.bfloat16)

    block_id = lax.broadcasted_iota(jnp.int32, (nbp, blk), 0)
    block_f = block_id.astype(jnp.float32)
    past = block_id < qi
    q_augs = []
    for g, hs in enumerate(heads):
        qt = q_ref[0, :, hs].astype(jnp.float32).T.astype(jnp.bfloat16)
        gate = _dot(kmh_ref[g, :nbp, :], qt) + _dot(kml_ref[g, :nbp, :], qt)
        gate = jnp.where(past, gate, -jnp.inf)
        attend = block_id == qi
        for _ in range(MOBA_TOPK):
            m = jnp.max(gate, axis=0, keepdims=True)
            first = jnp.min(jnp.where(gate == m, block_f, float(nbp)), axis=0, keepdims=True)
            hit = block_f == first
            attend = attend | (hit & past)
            gate = jnp.where(hit, -jnp.inf, gate)
        penalty = jnp.where(attend, 0.0, NEG_BIG).astype(jnp.bfloat16)
        q_augs.append(jnp.concatenate(
            [qt, penalty, jnp.zeros((LANES - nbp, blk), jnp.bfloat16)], axis=0))

    def scores(kj):
        start = pl.multiple_of(kj * blk, blk)
        return start, [_dot(kaug_ref[g, pl.ds(start, blk), :], q_augs[g]) for g in range(n_group)]

    key = lax.broadcasted_iota(jnp.int32, (blk, blk), 0)
    query = lax.broadcasted_iota(jnp.int32, (blk, blk), 1)
    start, ss = scores(qi)
    carry = []
    for g in range(n_group):
        s = jnp.where(key <= query, ss[g], NEG_BIG)
        m0 = jnp.max(s, axis=0, keepdims=True)
        p = jnp.exp2((s - m0) * exp2_scale)
        l0 = jnp.sum(p, axis=0, keepdims=True)
        carry.append((m0, l0, _dot(vt_ref[g, :, pl.ds(start, blk)], p.astype(jnp.bfloat16))))

    def body(kj, carry):
        start, ss = scores(kj)
        out = []
        for g in range(n_group):
            m_prev, l_prev, acc_prev = carry[g]
            m_new = jnp.maximum(m_prev, jnp.max(ss[g], axis=0, keepdims=True))
            alpha = jnp.exp2((m_prev - m_new) * exp2_scale)
            p = jnp.exp2((ss[g] - m_new) * exp2_scale)
            l_new = alpha * l_prev + jnp.sum(p, axis=0, keepdims=True)
            acc_new = alpha * acc_prev + _dot(vt_ref[g, :, pl.ds(start, blk)], p.astype(jnp.bfloat16))
            out.append((m_new, l_new, acc_new))
        return tuple(out)

    carry = lax.fori_loop(0, qi, body, tuple(carry))
    for g, hs in enumerate(heads):
        _, l_fin, acc = carry[g]
        o_ref[0, :, hs] = (acc / l_fin).T.astype(o_ref.dtype)


def _moba(qkv, n_heads, q_col, k_col, v_col):
    B, S, _ = qkv.shape
    blk, G = MOBA_BLOCK, MOBA_GROUP
    assert S % blk == 0 and S // blk <= LANES and blk & (blk - 1) == 0
    assert n_heads % G == 0 and q_col % G == 0 and k_col % G == 0 and v_col % G == 0
    W = G * HEAD_DIM
    return pl.pallas_call(
        functools.partial(_moba_kernel, G),
        grid=(B, n_heads // G, S // blk),
        in_specs=[
            pl.BlockSpec((1, blk, W), lambda b, h, i: (b, i, q_col // G + h)),
            pl.BlockSpec((1, S, W), lambda b, h, i: (b, 0, k_col // G + h)),
            pl.BlockSpec((1, S, W), lambda b, h, i: (b, 0, v_col // G + h)),
        ],
        out_specs=pl.BlockSpec((1, blk, W), lambda b, h, i: (b, i, h)),
        out_shape=jax.ShapeDtypeStruct((B, S, n_heads * HEAD_DIM), jnp.bfloat16),
        scratch_shapes=[pltpu.VMEM((G, S, HEAD_DIM + LANES), jnp.bfloat16),
                        pltpu.VMEM((G, HEAD_DIM, S), jnp.bfloat16),
                        pltpu.VMEM((G, LANES, HEAD_DIM), jnp.bfloat16),
                        pltpu.VMEM((G, LANES, HEAD_DIM), jnp.bfloat16)],
        compiler_params=pltpu.CompilerParams(
            dimension_semantics=("parallel", "parallel", "arbitrary")),
        name="moba",
    )(qkv, qkv, qkv)


def _sb_kernel(n_group, q_ref, k_ref, v_ref, tri_ref, o_ref, vt_ref):
    qi = pl.program_id(2)
    blk = SB_BLOCK
    S = k_ref.shape[1]
    scale = HEAD_DIM ** -0.5
    neg_tri = tri_ref[...]
    heads = [slice(g * HEAD_DIM, (g + 1) * HEAD_DIM) for g in range(n_group)]
    qts = [q_ref[0, :, hs].astype(jnp.float32).T.astype(jnp.bfloat16) for hs in heads]

    @pl.when(qi == 0)
    def _():
        for g, hs in enumerate(heads):
            for c in range(0, S, blk):
                vt_ref[g, :, c:c + blk] = v_ref[0, c:c + blk, hs].astype(jnp.float32).T.astype(vt_ref.dtype)

    def block(kj, carry, diagonal):
        start = pl.multiple_of(kj * blk, blk)
        if diagonal:
            key = lax.broadcasted_iota(jnp.int32, (blk, blk), 0)
            query = lax.broadcasted_iota(jnp.int32, (blk, blk), 1)
            causal = key < query
        zs = [_dot(k_ref[0, pl.ds(start, blk), hs], qts[g]) * scale for g, hs in enumerate(heads)]
        log_betas, col_sums, sticks = [], [], []
        for g in range(n_group):
            z = zs[g]
            softplus = jnp.maximum(z, 0.0) + jnp.log(1.0 + jnp.exp2(jnp.abs(z) * -LOG2_E))
            log_betas.append(z - softplus)
            if diagonal:
                softplus = jnp.where(causal, softplus, 0.0)
            col_sums.append(jnp.sum(softplus, axis=0, keepdims=True))
            sticks.append(_dot(neg_tri, softplus.astype(jnp.bfloat16)))
        out = []
        for g in range(n_group):
            rest, acc = carry[g]
            a = jnp.exp(log_betas[g] + sticks[g] + rest)
            if diagonal:
                a = jnp.where(causal, a, 0.0)
            acc = acc + _dot(vt_ref[g, :, pl.ds(start, blk)], a.astype(jnp.bfloat16))
            out.append((rest - col_sums[g], acc))
        return tuple(out)

    init = tuple((jnp.zeros((1, blk), jnp.float32), jnp.zeros((HEAD_DIM, blk), jnp.float32))
                 for _ in heads)
    def any_stick_left(carry):
        rest_max = functools.reduce(jnp.maximum, [rest for rest, _ in carry])
        return (jnp.max(rest_max) > SB_DEAD_LOG).astype(jnp.int32)

    def older_block(state):
        kj, _, carry = state
        carry = block(kj, carry, False)
        return kj - 1, any_stick_left(carry), carry

    carry = block(qi, init, True)
    _, _, carry = lax.while_loop(lambda s: (s[0] >= 0) & (s[1] > 0), older_block,
                                 (qi - 1, any_stick_left(carry), carry))
    for g, hs in enumerate(heads):
        o_ref[0, :, hs] = carry[g][1].T.astype(o_ref.dtype)


def _sb(qkv, n_heads, q_col, k_col, v_col):
    B, S, _ = qkv.shape
    blk, G = SB_BLOCK, SB_GROUP
    assert S % blk == 0 and n_heads % G == 0 and q_col % G == 0 and k_col % G == 0 and v_col % G == 0
    idx = jnp.arange(blk)
    neg_tri = -(idx[None, :] > idx[:, None]).astype(jnp.bfloat16)
    W = G * HEAD_DIM
    return pl.pallas_call(
        functools.partial(_sb_kernel, G),
        grid=(B, n_heads // G, S // blk),
        in_specs=[
            pl.BlockSpec((1, blk, W), lambda b, h, i: (b, i, q_col // G + h)),
            pl.BlockSpec((1, S, W), lambda b, h, i: (b, 0, k_col // G + h)),
            pl.BlockSpec((1, S, W), lambda b, h, i: (b, 0, v_col // G + h)),
            pl.BlockSpec((blk, blk), lambda b, h, i: (0, 0)),
        ],
        out_specs=pl.BlockSpec((1, blk, W), lambda b, h, i: (b, i, h)),
        out_shape=jax.ShapeDtypeStruct((B, S, n_heads * HEAD_DIM), jnp.bfloat16),
        scratch_shapes=[pltpu.VMEM((G, HEAD_DIM, S), jnp.bfloat16)],
        compiler_params=pltpu.CompilerParams(
            dimension_semantics=("parallel", "parallel", "arbitrary")),
        name="sb",
    )(qkv, qkv, qkv, neg_tri)


def _out_kernel(x_ref, oa_ref, ob_ref, ga_ref, gb_ref, wa_ref, wb_ref, g_ref, next_g_ref, o_ref, on_ref):
    na = _rms(oa_ref[...].astype(jnp.float32), ga_ref[...]).astype(jnp.bfloat16)
    nb = _rms(ob_ref[...].astype(jnp.float32), gb_ref[...]).astype(jnp.bfloat16)
    y = x_ref[...] + _rms(_dot(na, wa_ref[...]) + _dot(nb, wb_ref[...]), g_ref[...])
    o_ref[...] = y
    on_ref[...] = _rms(y, next_g_ref[...]).astype(on_ref.dtype)


def _out(x, oa, ob, ga, gb, w, g, next_g):
    T, D = x.shape
    Wa, Wb = oa.shape[1], ob.shape[1]
    assert Wa == Wb and w.shape[0] == Wa + Wb
    tm = OUT_TM
    assert T % tm == 0
    return pl.pallas_call(
        _out_kernel,
        grid=(T // tm,),
        in_specs=[
            pl.BlockSpec((tm, D), lambda i: (i, 0)),
            pl.BlockSpec((tm, Wa), lambda i: (i, 0)),
            pl.BlockSpec((tm, Wb), lambda i: (i, 0)),
            pl.BlockSpec((1, Wa), lambda i: (0, 0)),
            pl.BlockSpec((1, Wb), lambda i: (0, 0)),
            pl.BlockSpec((Wa, D), lambda i: (0, 0)),
            pl.BlockSpec((Wb, D), lambda i: (1, 0)),
            pl.BlockSpec((1, D), lambda i: (0, 0)),
            pl.BlockSpec((1, D), lambda i: (0, 0)),
        ],
        out_specs=[pl.BlockSpec((tm, D), lambda i: (i, 0)), pl.BlockSpec((tm, D), lambda i: (i, 0))],
        out_shape=[jax.ShapeDtypeStruct((T, D), jnp.float32), jax.ShapeDtypeStruct((T, D), jnp.bfloat16)],
        compiler_params=pltpu.CompilerParams(
            dimension_semantics=("parallel",), vmem_limit_bytes=VMEM_LIMIT),
        name="out_proj",
    )(x, oa, ob, ga, gb, w, w, g, next_g)


def kernel(x, ffn1_pre_g, ffn1_w_gate, ffn1_w_up, ffn1_w_down, ffn1_post_g, mix_pre_g, w_in, moba_out_g, sb_out_g, w_out, mix_post_g, ffn2_pre_g, ffn2_w_gate, ffn2_w_up, ffn2_w_down, ffn2_post_g):
    B, S, D = x.shape
    depth = w_in.shape[0]
    w_moba = moba_out_g.shape[1]
    w_sb = sb_out_g.shape[1]
    h_moba, h_sb = w_moba // HEAD_DIM, w_sb // HEAD_DIM
    tables = _rope_tables(S)

    xt = x.reshape(B * S, D)
    for l in range(depth):
        h, (wd1,) = _ffn_up(_prenorm(xt, ffn1_pre_g[l:l + 1]), ffn1_w_gate[l], ffn1_w_up[l],
                            [ffn1_w_down[l]])
        xt, xn, (w_in_bf, w_out_bf) = _ffn_down(h, wd1, xt, ffn1_post_g[l:l + 1], mix_pre_g[l:l + 1],
                                                [w_in[l], w_out[l]])
        qkv = _proj(xn, w_in_bf, tables, S, 2 * w_moba).reshape(B, S, -1)
        o_a = _moba(qkv, h_moba, 0, h_moba, 2 * h_moba)
        o_b = _sb(qkv, h_sb, 3 * h_moba, 3 * h_moba + h_sb, 3 * h_moba + 2 * h_sb)
        xt, xn = _out(xt, o_a.reshape(B * S, w_moba), o_b.reshape(B * S, w_sb),
                      moba_out_g[l:l + 1], sb_out_g[l:l + 1], w_out_bf, mix_post_g[l:l + 1],
                      ffn2_pre_g[l:l + 1])
        h, (wd2,) = _ffn_up(xn, ffn2_w_gate[l], ffn2_w_up[l], [ffn2_w_down[l]])
        xt, _, _ = _ffn_down(h, wd2, xt, ffn2_post_g[l:l + 1], None, [])
    return xt.reshape(B, S, D)
```

```python
import functools

import jax
import jax.numpy as jnp
from jax import lax
from jax.experimental import pallas as pl
from jax.experimental.pallas import tpu as pltpu

HEAD_DIM = 128
MOBA_BLOCK = 256
MOBA_TOPK = 3
ROPE_THETA = 500000.0
ROT_DIM = HEAD_DIM // 4
RMS_EPS = 1e-6
FFN_RES_SCALE = 0.5

LANES = 128
BF16_ROWS = 16
LOG2_E = 1.4426950408889634
NEG_BIG = -1e30
MOBA_GROUP = 8
SB_BLOCK = 256
SB_DEAD_LOG = -104.0
SB_GROUP = 8
VMEM_LIMIT = 56 * 1024 * 1024

NORM_TM = 1024
UP_TM, UP_TF = 2048, 512
DOWN_TM = 256
PROJ_TM, PROJ_TN = 1024, 1024
MXU_COLS = 256
OUT_TM = 512


def _rms(xf, g):
    return xf * lax.rsqrt(jnp.mean(xf * xf, axis=-1, keepdims=True) + RMS_EPS) * g


def _dot(a, b):
    return jnp.dot(a, b, preferred_element_type=jnp.float32)


def _dot_nt(a, b):
    return lax.dot_general(a, b, (((1,), (1,)), ((), ())), preferred_element_type=jnp.float32)


def _prenorm_kernel(x_ref, g_ref, o_ref):
    o_ref[...] = _rms(x_ref[...], g_ref[...]).astype(o_ref.dtype)


def _prenorm(x, g):
    T, D = x.shape
    tm = NORM_TM
    assert T % tm == 0
    return pl.pallas_call(
        _prenorm_kernel,
        grid=(T // tm,),
        in_specs=[pl.BlockSpec((tm, D), lambda i: (i, 0)), pl.BlockSpec((1, D), lambda i: (0, 0))],
        out_specs=pl.BlockSpec((tm, D), lambda i: (i, 0)),
        out_shape=jax.ShapeDtypeStruct((T, D), jnp.bfloat16),
        compiler_params=pltpu.CompilerParams(
            dimension_semantics=("parallel",), vmem_limit_bytes=VMEM_LIMIT),
        name="prenorm",
    )(x, g)


def _cast_specs(weights, n_steps, step_index):
    specs, shapes = [], []
    for w in weights:
        rows, cols = w.shape
        assert rows % (n_steps * BF16_ROWS) == 0
        specs.append(pl.BlockSpec((rows // n_steps, cols), step_index))
        shapes.append(jax.ShapeDtypeStruct(w.shape, jnp.bfloat16))
    return specs, shapes


def _cast_blocks(src_refs, dst_refs):
    for src, dst in zip(src_refs, dst_refs):
        dst[...] = src[...].astype(dst.dtype)


def _ffn_up_kernel(n_cast, xn_ref, wg_ref, wu_ref, *refs):
    h_ref = refs[n_cast]
    _cast_blocks(refs[:n_cast], refs[n_cast + 1:])
    xn = xn_ref[...]
    cols = [slice(c, c + MXU_COLS) for c in range(0, h_ref.shape[1], MXU_COLS)]
    gu = [(_dot(xn, wg_ref[:, cs].astype(jnp.bfloat16)), _dot(xn, wu_ref[:, cs].astype(jnp.bfloat16)))
          for cs in cols]
    for cs, (g, u) in zip(cols, gu):
        h_ref[:, cs] = (g * jax.nn.sigmoid(g) * u).astype(h_ref.dtype)


def _ffn_up(xn, wg, wu, to_cast):
    T, D = xn.shape
    F = wg.shape[1]
    tm, tf = UP_TM, UP_TF
    assert T % tm == 0 and F % tf == 0 and tf % MXU_COLS == 0
    n_i, n_j = T // tm, F // tf
    w_spec = pl.BlockSpec((D, tf), lambda i, j: (0, j))
    cast_specs, cast_shapes = _cast_specs(to_cast, n_i * n_j, lambda i, j: (i * n_j + j, 0))
    outs = pl.pallas_call(
        functools.partial(_ffn_up_kernel, len(to_cast)),
        grid=(n_i, n_j),
        in_specs=[pl.BlockSpec((tm, D), lambda i, j: (i, 0)), w_spec, w_spec] + cast_specs,
        out_specs=[pl.BlockSpec((tm, tf), lambda i, j: (i, j))] + cast_specs,
        out_shape=[jax.ShapeDtypeStruct((T, F), jnp.bfloat16)] + cast_shapes,
        compiler_params=pltpu.CompilerParams(
            dimension_semantics=("arbitrary", "arbitrary"), vmem_limit_bytes=VMEM_LIMIT),
        name="ffn_up",
    )(xn, wg, wu, *to_cast)
    return outs[0], outs[1:]


def _ffn_down_kernel(emit_next, n_cast, h_ref, wd_ref, x_ref, post_g_ref, *refs):
    n_in = n_cast + emit_next
    n_out = 1 + emit_next
    _cast_blocks(refs[emit_next:n_in], refs[n_in + n_out:])
    o_ref = refs[n_in]
    y = x_ref[...] + FFN_RES_SCALE * _rms(_dot(h_ref[...], wd_ref[...]), post_g_ref[...])
    if emit_next:
        next_g_ref, on_ref = refs[0], refs[n_in + 1]
        on_ref[...] = _rms(y, next_g_ref[...]).astype(on_ref.dtype)
    o_ref[...] = y


def _ffn_down(h, wd, x, post_g, next_g, to_cast):
    T, F = h.shape
    D = wd.shape[1]
    tm = DOWN_TM
    assert T % tm == 0
    emit_next = next_g is not None
    row_spec = pl.BlockSpec((tm, D), lambda i: (i, 0))
    gain_spec = pl.BlockSpec((1, D), lambda i: (0, 0))
    cast_specs, cast_shapes = _cast_specs(to_cast, T // tm, lambda i: (i, 0))
    in_specs = [pl.BlockSpec((tm, F), lambda i: (i, 0)),
                pl.BlockSpec((F, D), lambda i: (0, 0), pipeline_mode=pl.Buffered(1)),
                row_spec, gain_spec]
    args = [h, wd, x, post_g]
    out_specs, out_shape = [row_spec], [jax.ShapeDtypeStruct((T, D), jnp.float32)]
    if emit_next:
        in_specs.append(gain_spec)
        args.append(next_g)
        out_specs.append(row_spec)
        out_shape.append(jax.ShapeDtypeStruct((T, D), jnp.bfloat16))
    outs = pl.pallas_call(
        functools.partial(_ffn_down_kernel, emit_next, len(to_cast)),
        grid=(T // tm,),
        in_specs=in_specs + cast_specs,
        out_specs=out_specs + cast_specs,
        out_shape=out_shape + cast_shapes,
        compiler_params=pltpu.CompilerParams(
            dimension_semantics=("arbitrary",), vmem_limit_bytes=VMEM_LIMIT),
        name="ffn_down",
    )(*args, *to_cast)
    n_out = 1 + emit_next
    return outs[0], (outs[1] if emit_next else None), outs[n_out:]


def _proj_kernel(n_rope_tiles, xn_ref, w_ref, cos_ref, sa_ref, sb_ref, o_ref):
    j = pl.program_id(1)

    def tile(rope):
        xn = xn_ref[...]
        ys = [_dot(xn, w_ref[:, c:c + MXU_COLS]) for c in range(0, w_ref.shape[1], MXU_COLS)]
        for n, y in enumerate(ys):
            for h in range(0, MXU_COLS, HEAD_DIM):
                yh = y[:, h:h + HEAD_DIM]
                if rope:
                    yh = (yh * cos_ref[...] + pltpu.roll(yh, HEAD_DIM - ROT_DIM // 2, 1) * sa_ref[...]
                          + pltpu.roll(yh, ROT_DIM // 2, 1) * sb_ref[...])
                o_ref[:, n * MXU_COLS + h:n * MXU_COLS + h + HEAD_DIM] = yh.astype(o_ref.dtype)

    pl.when(j < n_rope_tiles)(functools.partial(tile, True))
    pl.when(j >= n_rope_tiles)(functools.partial(tile, False))


def _proj(xn, w, tables, seq, n_rope_cols):
    T, D = xn.shape
    N = w.shape[1]
    tm, tn = PROJ_TM, PROJ_TN
    assert T % tm == 0 and N % tn == 0 and seq % tm == 0 and n_rope_cols % tn == 0 and tn % MXU_COLS == 0
    pos_tiles = seq // tm
    tab_spec = pl.BlockSpec((tm, HEAD_DIM), lambda i, j: (i % pos_tiles, 0))
    return pl.pallas_call(
        functools.partial(_proj_kernel, n_rope_cols // tn),
        grid=(T // tm, N // tn),
        in_specs=[
            pl.BlockSpec((tm, D), lambda i, j: (i, 0)),
            pl.BlockSpec((D, tn), lambda i, j: (0, j)),
            tab_spec, tab_spec, tab_spec,
        ],
        out_specs=pl.BlockSpec((tm, tn), lambda i, j: (i, j)),
        out_shape=jax.ShapeDtypeStruct((T, N), jnp.bfloat16),
        compiler_params=pltpu.CompilerParams(
            dimension_semantics=("parallel", "arbitrary"), vmem_limit_bytes=VMEM_LIMIT),
        name="proj",
    )(xn, w, *tables)


def _rope_tables(seq):
    half = ROT_DIM // 2
    inv_freq = ROPE_THETA ** (-jnp.arange(0, ROT_DIM, 2, dtype=jnp.float32) / ROT_DIM)
    ang = jnp.arange(seq, dtype=jnp.float32)[:, None] * inv_freq[None, :]
    cos, sin = jnp.cos(ang), jnp.sin(ang)
    zeros = jnp.zeros((seq, HEAD_DIM - half), jnp.float32)
    cos_t = jnp.concatenate([cos, cos, jnp.ones((seq, HEAD_DIM - ROT_DIM), jnp.float32)], axis=1)
    sa_t = jnp.concatenate([-sin, zeros], axis=1)
    sb_t = jnp.concatenate([jnp.zeros((seq, half), jnp.float32), sin,
                            jnp.zeros((seq, HEAD_DIM - ROT_DIM), jnp.float32)], axis=1)
    return cos_t, sa_t, sb_t


def _moba_kernel(n_group, q_ref, k_ref, v_ref, o_ref, kaug_ref, vt_ref, kmh_ref, kml_ref):
    qi = pl.program_id(2)
    blk = MOBA_BLOCK
    S = k_ref.shape[1]
    nb = S // blk
    nbp = -(-nb // BF16_ROWS) * BF16_ROWS
    exp2_scale = HEAD_DIM ** -0.5 * LOG2_E
    heads = [slice(g * HEAD_DIM, (g + 1) * HEAD_DIM) for g in range(n_group)]

    @pl.when(qi == 0)
    def _():
        key_block = lax.shift_right_logical(lax.broadcasted_iota(jnp.int32, (S, LANES), 0),
                                            blk.bit_length() - 1)
        one_hot = (key_block == lax.broadcasted_iota(jnp.int32, (S, LANES), 1)).astype(jnp.bfloat16)
        kmh_ref[...] = jnp.zeros_like(kmh_ref)
        kml_ref[...] = jnp.zeros_like(kml_ref)
        for g, hs in enumerate(heads):
            kaug_ref[g, :, :HEAD_DIM] = k_ref[0, :, hs]
            kaug_ref[g, :, HEAD_DIM:] = one_hot
            for c in range(0, S, blk):
                vt_ref[g, :, c:c + blk] = v_ref[0, c:c + blk, hs].astype(jnp.float32).T.astype(vt_ref.dtype)
            for n in range(nb):
                km = jnp.mean(k_ref[0, n * blk:(n + 1) * blk, hs].astype(jnp.float32), axis=0, keepdims=True)
                hi = km.astype(jnp.bfloat16)
                kmh_ref[g, n:n + 1, :] = hi
                kml_ref[g, n:n + 1, :] = (km - hi.astype(jnp.float32)).astype(jnp.bfloat16)

    block_id = lax.broadcasted_iota(jnp.int32, (nbp, blk), 0)
    block_f = block_id.astype(jnp.float32)
    past = block_id < qi
    q_augs = []
    for g, hs in enumerate(heads):
        qt = q_ref[0, :, hs].astype(jnp.float32).T.astype(jnp.bfloat16)
        gate = _dot(kmh_ref[g, :nbp, :], qt) + _dot(kml_ref[g, :nbp, :], qt)
        gate = jnp.where(past, gate, -jnp.inf)
        attend = block_id == qi
        for _ in range(MOBA_TOPK):
            m = jnp.max(gate, axis=0, keepdims=True)
            first = jnp.min(jnp.where(gate == m, block_f, float(nbp)), axis=0, keepdims=True)
            hit = block_f == first
            attend = attend | (hit & past)
            gate = jnp.where(hit, -jnp.inf, gate)
        penalty = jnp.where(attend, 0.0, NEG_BIG).astype(jnp.bfloat16)
        q_augs.append(jnp.concatenate(
            [qt, penalty, jnp.zeros((LANES - nbp, blk), jnp.bfloat16)], axis=0))

    def scores(kj, n_keys=blk):
        start = pl.multiple_of(kj * blk, blk)
        return start, [_dot(kaug_ref[g, pl.ds(start, n_keys), :], q_augs[g]) for g in range(n_group)]

    key = lax.broadcasted_iota(jnp.int32, (blk, blk), 0)
    query = lax.broadcasted_iota(jnp.int32, (blk, blk), 1)
    start, ss = scores(qi)
    carry = []
    for g in range(n_group):
        s = jnp.where(key <= query, ss[g], NEG_BIG)
        m0 = jnp.max(s, axis=0, keepdims=True)
        p = jnp.exp2((s - m0) * exp2_scale)
        l0 = jnp.sum(p, axis=0, keepdims=True)
        carry.append((m0, l0, _dot(vt_ref[g, :, pl.ds(start, blk)], p.astype(jnp.bfloat16))))

    def past_keys(kj, carry, n_keys):
        start, ss = scores(kj, n_keys)
        out = []
        for g in range(n_group):
            m_prev, l_prev, acc_prev = carry[g]
            m_new = jnp.maximum(m_prev, jnp.max(ss[g], axis=0, keepdims=True))
            alpha = jnp.exp2((m_prev - m_new) * exp2_scale)
            p = jnp.exp2((ss[g] - m_new) * exp2_scale)
            l_new = alpha * l_prev + jnp.sum(p, axis=0, keepdims=True)
            acc_new = alpha * acc_prev + _dot(vt_ref[g, :, pl.ds(start, n_keys)], p.astype(jnp.bfloat16))
            out.append((m_new, l_new, acc_new))
        return tuple(out)

    carry = lax.fori_loop(0, qi // 2, lambda i, c: past_keys(2 * i, c, 2 * blk), tuple(carry))
    carry = lax.cond(qi % 2 == 1, lambda c: past_keys(qi - 1, c, blk), lambda c: c, carry)
    for g, hs in enumerate(heads):
        _, l_fin, acc = carry[g]
        o_ref[0, :, hs] = (acc / l_fin).T.astype(o_ref.dtype)


def _moba(qkv, n_heads, q_col, k_col, v_col):
    B, S, _ = qkv.shape
    blk, G = MOBA_BLOCK, MOBA_GROUP
    assert S % blk == 0 and S // blk <= LANES and blk & (blk - 1) == 0
    assert n_heads % G == 0 and q_col % G == 0 and k_col % G == 0 and v_col % G == 0
    W = G * HEAD_DIM
    return pl.pallas_call(
        functools.partial(_moba_kernel, G),
        grid=(B, n_heads // G, S // blk),
        in_specs=[
            pl.BlockSpec((1, blk, W), lambda b, h, i: (b, i, q_col // G + h)),
            pl.BlockSpec((1, S, W), lambda b, h, i: (b, 0, k_col // G + h)),
            pl.BlockSpec((1, S, W), lambda b, h, i: (b, 0, v_col // G + h)),
        ],
        out_specs=pl.BlockSpec((1, blk, W), lambda b, h, i: (b, i, h)),
        out_shape=jax.ShapeDtypeStruct((B, S, n_heads * HEAD_DIM), jnp.bfloat16),
        scratch_shapes=[pltpu.VMEM((G, S, HEAD_DIM + LANES), jnp.bfloat16),
                        pltpu.VMEM((G, HEAD_DIM, S), jnp.bfloat16),
                        pltpu.VMEM((G, LANES, HEAD_DIM), jnp.bfloat16),
                        pltpu.VMEM((G, LANES, HEAD_DIM), jnp.bfloat16)],
        compiler_params=pltpu.CompilerParams(
            dimension_semantics=("parallel", "parallel", "arbitrary")),
        name="moba",
    )(qkv, qkv, qkv)


def _sb_kernel(n_group, q_ref, k_ref, v_ref, tri_ref, o_ref, vt_ref):
    qi = pl.program_id(2)
    blk = SB_BLOCK
    S = k_ref.shape[1]
    scale = HEAD_DIM ** -0.5
    neg_tri = tri_ref[...]
    heads = [slice(g * HEAD_DIM, (g + 1) * HEAD_DIM) for g in range(n_group)]
    qts = [q_ref[0, :, hs].astype(jnp.float32).T.astype(jnp.bfloat16) for hs in heads]

    @pl.when(qi == 0)
    def _():
        for g, hs in enumerate(heads):
            for c in range(0, S, blk):
                vt_ref[g, :, c:c + blk] = v_ref[0, c:c + blk, hs].astype(jnp.float32).T.astype(vt_ref.dtype)

    def block(kj, carry, diagonal):
        start = pl.multiple_of(kj * blk, blk)
        if diagonal:
            key = lax.broadcasted_iota(jnp.int32, (blk, blk), 0)
            query = lax.broadcasted_iota(jnp.int32, (blk, blk), 1)
            causal = key < query
        zs = [_dot(k_ref[0, pl.ds(start, blk), hs], qts[g]) * scale for g, hs in enumerate(heads)]
        log_betas, col_sums, sticks = [], [], []
        for g in range(n_group):
            z = zs[g]
            softplus = jnp.maximum(z, 0.0) + jnp.log(1.0 + jnp.exp2(jnp.abs(z) * -LOG2_E))
            log_betas.append(z - softplus)
            if diagonal:
                softplus = jnp.where(causal, softplus, 0.0)
            col_sums.append(jnp.sum(softplus, axis=0, keepdims=True))
            sticks.append(_dot(neg_tri, softplus.astype(jnp.bfloat16)))
        out = []
        for g in range(n_group):
            rest, acc = carry[g]
            a = jnp.exp(log_betas[g] + sticks[g] + rest)
            if diagonal:
                a = jnp.where(causal, a, 0.0)
            acc = acc + _dot(vt_ref[g, :, pl.ds(start, blk)], a.astype(jnp.bfloat16))
            out.append((rest - col_sums[g], acc))
        return tuple(out)

    init = tuple((jnp.zeros((1, blk), jnp.float32), jnp.zeros((HEAD_DIM, blk), jnp.float32))
                 for _ in heads)
    def any_stick_left(carry):
        rest_max = functools.reduce(jnp.maximum, [rest for rest, _ in carry])
        return (jnp.max(rest_max) > SB_DEAD_LOG).astype(jnp.int32)

    def older_block(state):
        kj, _, carry = state
        carry = block(kj, carry, False)
        return kj - 1, any_stick_left(carry), carry

    carry = block(qi, init, True)
    _, _, carry = lax.while_loop(lambda s: (s[0] >= 0) & (s[1] > 0), older_block,
                                 (qi - 1, any_stick_left(carry), carry))
    for g, hs in enumerate(heads):
        o_ref[0, :, hs] = carry[g][1].T.astype(o_ref.dtype)


def _sb(qkv, n_heads, q_col, k_col, v_col):
    B, S, _ = qkv.shape
    blk, G = SB_BLOCK, SB_GROUP
    assert S % blk == 0 and n_heads % G == 0 and q_col % G == 0 and k_col % G == 0 and v_col % G == 0
    idx = jnp.arange(blk)
    neg_tri = -(idx[None, :] > idx[:, None]).astype(jnp.bfloat16)
    W = G * HEAD_DIM
    return pl.pallas_call(
        functools.partial(_sb_kernel, G),
        grid=(B, n_heads // G, S // blk),
        in_specs=[
            pl.BlockSpec((1, blk, W), lambda b, h, i: (b, i, q_col // G + h)),
            pl.BlockSpec((1, S, W), lambda b, h, i: (b, 0, k_col // G + h)),
            pl.BlockSpec((1, S, W), lambda b, h, i: (b, 0, v_col // G + h)),
            pl.BlockSpec((blk, blk), lambda b, h, i: (0, 0)),
        ],
        out_specs=pl.BlockSpec((1, blk, W), lambda b, h, i: (b, i, h)),
        out_shape=jax.ShapeDtypeStruct((B, S, n_heads * HEAD_DIM), jnp.bfloat16),
        scratch_shapes=[pltpu.VMEM((G, HEAD_DIM, S), jnp.bfloat16)],
        compiler_params=pltpu.CompilerParams(
            dimension_semantics=("parallel", "parallel", "arbitrary")),
        name="sb",
    )(qkv, qkv, qkv, neg_tri)


def _out_kernel(x_ref, oa_ref, ob_ref, ga_ref, gb_ref, wa_ref, wb_ref, g_ref, next_g_ref, o_ref, on_ref):
    half = x_ref.shape[0] // 2
    rows = [slice(0, half), slice(half, 2 * half)]
    ns = [(_rms(oa_ref[rs, :].astype(jnp.float32), ga_ref[...]).astype(jnp.bfloat16),
           _rms(ob_ref[rs, :].astype(jnp.float32), gb_ref[...]).astype(jnp.bfloat16)) for rs in rows]
    fs = [_dot(na, wa_ref[...]) + _dot(nb, wb_ref[...]) for na, nb in ns]
    for rs, f in zip(rows, fs):
        y = x_ref[rs, :] + _rms(f, g_ref[...])
        o_ref[rs, :] = y
        on_ref[rs, :] = _rms(y, next_g_ref[...]).astype(on_ref.dtype)


def _out(x, oa, ob, ga, gb, w, g, next_g):
    T, D = x.shape
    Wa, Wb = oa.shape[1], ob.shape[1]
    assert Wa == Wb and w.shape[0] == Wa + Wb
    tm = OUT_TM
    assert T % tm == 0
    return pl.pallas_call(
        _out_kernel,
        grid=(T // tm,),
        in_specs=[
            pl.BlockSpec((tm, D), lambda i: (i, 0)),
            pl.BlockSpec((tm, Wa), lambda i: (i, 0)),
            pl.BlockSpec((tm, Wb), lambda i: (i, 0)),
            pl.BlockSpec((1, Wa), lambda i: (0, 0)),
            pl.BlockSpec((1, Wb), lambda i: (0, 0)),
            pl.BlockSpec((Wa, D), lambda i: (0, 0)),
            pl.BlockSpec((Wb, D), lambda i: (1, 0)),
            pl.BlockSpec((1, D), lambda i: (0, 0)),
            pl.BlockSpec((1, D), lambda i: (0, 0)),
        ],
        out_specs=[pl.BlockSpec((tm, D), lambda i: (i, 0)), pl.BlockSpec((tm, D), lambda i: (i, 0))],
        out_shape=[jax.ShapeDtypeStruct((T, D), jnp.float32), jax.ShapeDtypeStruct((T, D), jnp.bfloat16)],
        compiler_params=pltpu.CompilerParams(
            dimension_semantics=("parallel",), vmem_limit_bytes=VMEM_LIMIT),
        name="out_proj",
    )(x, oa, ob, ga, gb, w, w, g, next_g)


def kernel(x, ffn1_pre_g, ffn1_w_gate, ffn1_w_up, ffn1_w_down, ffn1_post_g, mix_pre_g, w_in, moba_out_g, sb_out_g, w_out, mix_post_g, ffn2_pre_g, ffn2_w_gate, ffn2_w_up, ffn2_w_down, ffn2_post_g):
    B, S, D = x.shape
    depth = w_in.shape[0]
    w_moba = moba_out_g.shape[1]
    w_sb = sb_out_g.shape[1]
    h_moba, h_sb = w_moba // HEAD_DIM, w_sb // HEAD_DIM
    tables = _rope_tables(S)

    xt = x.reshape(B * S, D)
    for l in range(depth):
        h, (wd1,) = _ffn_up(_prenorm(xt, ffn1_pre_g[l:l + 1]), ffn1_w_gate[l], ffn1_w_up[l],
                            [ffn1_w_down[l]])
        xt, xn, (w_in_bf, w_out_bf) = _ffn_down(h, wd1, xt, ffn1_post_g[l:l + 1], mix_pre_g[l:l + 1],
                                                [w_in[l], w_out[l]])
        qkv = _proj(xn, w_in_bf, tables, S, 2 * w_moba).reshape(B, S, -1)
        o_a = _moba(qkv, h_moba, 0, h_moba, 2 * h_moba)
        o_b = _sb(qkv, h_sb, 3 * h_moba, 3 * h_moba + h_sb, 3 * h_moba + 2 * h_sb)
        xt, xn = _out(xt, o_a.reshape(B * S, w_moba), o_b.reshape(B * S, w_sb),
                      moba_out_g[l:l + 1], sb_out_g[l:l + 1], w_out_bf, mix_post_g[l:l + 1],
                      ffn2_pre_g[l:l + 1])
        h, (wd2,) = _ffn_up(xn, ffn2_w_gate[l], ffn2_w_up[l], [ffn2_w_down[l]])
        xt, _, _ = _ffn_down(h, wd2, xt, ffn2_post_g[l:l + 1], None, [])
    return xt.reshape(B, S, D)
```

```python
import functools

import jax
import jax.numpy as jnp
from jax import lax
from jax.experimental import pallas as pl
from jax.experimental.pallas import tpu as pltpu

HEAD_DIM = 128
MOBA_BLOCK = 256
MOBA_TOPK = 3
ROPE_THETA = 500000.0
ROT_DIM = HEAD_DIM // 4
RMS_EPS = 1e-6
FFN_RES_SCALE = 0.5

LANES = 128
BF16_ROWS = 16
LOG2_E = 1.4426950408889634
NEG_BIG = -1e30
MOBA_GROUP = 8
SB_BLOCK = 256
SB_DEAD_LOG = -104.0
SB_GROUP = 8
VMEM_LIMIT = 56 * 1024 * 1024

NORM_TM = 1024
UP_TM, UP_TF = 2048, 512
DOWN_TM = 256
PROJ_TM, PROJ_TN = 1024, 2048
MXU_COLS = 256
OUT_TM = 512


def _rms(xf, g):
    return xf * lax.rsqrt(jnp.mean(xf * xf, axis=-1, keepdims=True) + RMS_EPS) * g


def _dot(a, b):
    return jnp.dot(a, b, preferred_element_type=jnp.float32)


def _dot_nt(a, b):
    return lax.dot_general(a, b, (((1,), (1,)), ((), ())), preferred_element_type=jnp.float32)


def _prenorm_kernel(x_ref, g_ref, o_ref):
    o_ref[...] = _rms(x_ref[...], g_ref[...]).astype(o_ref.dtype)


def _prenorm(x, g):
    T, D = x.shape
    tm = NORM_TM
    assert T % tm == 0
    return pl.pallas_call(
        _prenorm_kernel,
        grid=(T // tm,),
        in_specs=[pl.BlockSpec((tm, D), lambda i: (i, 0)), pl.BlockSpec((1, D), lambda i: (0, 0))],
        out_specs=pl.BlockSpec((tm, D), lambda i: (i, 0)),
        out_shape=jax.ShapeDtypeStruct((T, D), jnp.bfloat16),
        compiler_params=pltpu.CompilerParams(
            dimension_semantics=("parallel",), vmem_limit_bytes=VMEM_LIMIT),
        name="prenorm",
    )(x, g)


def _cast_specs(weights, n_steps, step_index):
    specs, shapes = [], []
    for w in weights:
        rows, cols = w.shape
        assert rows % (n_steps * BF16_ROWS) == 0
        specs.append(pl.BlockSpec((rows // n_steps, cols), step_index))
        shapes.append(jax.ShapeDtypeStruct(w.shape, jnp.bfloat16))
    return specs, shapes


def _cast_blocks(src_refs, dst_refs):
    for src, dst in zip(src_refs, dst_refs):
        dst[...] = src[...].astype(dst.dtype)


def _ffn_up_kernel(n_cast, xn_ref, wg_ref, wu_ref, *refs):
    h_ref = refs[n_cast]
    _cast_blocks(refs[:n_cast], refs[n_cast + 1:])
    xn = xn_ref[...]
    cols = [slice(c, c + MXU_COLS) for c in range(0, h_ref.shape[1], MXU_COLS)]
    gu = [(_dot(xn, wg_ref[:, cs].astype(jnp.bfloat16)), _dot(xn, wu_ref[:, cs].astype(jnp.bfloat16)))
          for cs in cols]
    for cs, (g, u) in zip(cols, gu):
        h_ref[:, cs] = (g * jax.nn.sigmoid(g) * u).astype(h_ref.dtype)


def _ffn_up(xn, wg, wu, to_cast):
    T, D = xn.shape
    F = wg.shape[1]
    tm, tf = UP_TM, UP_TF
    assert T % tm == 0 and F % tf == 0 and tf % MXU_COLS == 0
    n_i, n_j = T // tm, F // tf
    w_spec = pl.BlockSpec((D, tf), lambda i, j: (0, j))
    cast_specs, cast_shapes = _cast_specs(to_cast, n_i * n_j, lambda i, j: (i * n_j + j, 0))
    outs = pl.pallas_call(
        functools.partial(_ffn_up_kernel, len(to_cast)),
        grid=(n_i, n_j),
        in_specs=[pl.BlockSpec((tm, D), lambda i, j: (i, 0)), w_spec, w_spec] + cast_specs,
        out_specs=[pl.BlockSpec((tm, tf), lambda i, j: (i, j))] + cast_specs,
        out_shape=[jax.ShapeDtypeStruct((T, F), jnp.bfloat16)] + cast_shapes,
        compiler_params=pltpu.CompilerParams(
            dimension_semantics=("arbitrary", "arbitrary"), vmem_limit_bytes=VMEM_LIMIT),
        name="ffn_up",
    )(xn, wg, wu, *to_cast)
    return outs[0], outs[1:]


def _ffn_down_kernel(emit_next, n_cast, h_ref, wd_ref, x_ref, post_g_ref, *refs):
    n_in = n_cast + emit_next
    n_out = 1 + emit_next
    _cast_blocks(refs[emit_next:n_in], refs[n_in + n_out:])
    o_ref = refs[n_in]
    y = x_ref[...] + FFN_RES_SCALE * _rms(_dot(h_ref[...], wd_ref[...]), post_g_ref[...])
    if emit_next:
        next_g_ref, on_ref = refs[0], refs[n_in + 1]
        on_ref[...] = _rms(y, next_g_ref[...]).astype(on_ref.dtype)
    o_ref[...] = y


def _ffn_down(h, wd, x, post_g, next_g, to_cast):
    T, F = h.shape
    D = wd.shape[1]
    tm = DOWN_TM
    assert T % tm == 0
    emit_next = next_g is not None
    row_spec = pl.BlockSpec((tm, D), lambda i: (i, 0))
    gain_spec = pl.BlockSpec((1, D), lambda i: (0, 0))
    cast_specs, cast_shapes = _cast_specs(to_cast, T // tm, lambda i: (i, 0))
    in_specs = [pl.BlockSpec((tm, F), lambda i: (i, 0)),
                pl.BlockSpec((F, D), lambda i: (0, 0), pipeline_mode=pl.Buffered(1)),
                row_spec, gain_spec]
    args = [h, wd, x, post_g]
    out_specs, out_shape = [row_spec], [jax.ShapeDtypeStruct((T, D), jnp.float32)]
    if emit_next:
        in_specs.append(gain_spec)
        args.append(next_g)
        out_specs.append(row_spec)
        out_shape.append(jax.ShapeDtypeStruct((T, D), jnp.bfloat16))
    outs = pl.pallas_call(
        functools.partial(_ffn_down_kernel, emit_next, len(to_cast)),
        grid=(T // tm,),
        in_specs=in_specs + cast_specs,
        out_specs=out_specs + cast_specs,
        out_shape=out_shape + cast_shapes,
        compiler_params=pltpu.CompilerParams(
            dimension_semantics=("arbitrary",), vmem_limit_bytes=VMEM_LIMIT),
        name="ffn_down",
    )(*args, *to_cast)
    n_out = 1 + emit_next
    return outs[0], (outs[1] if emit_next else None), outs[n_out:]


def _proj_kernel(n_rope_tiles, xn_ref, w_ref, cos_ref, sa_ref, sb_ref, o_ref):
    j = pl.program_id(1)

    def tile(rope):
        xn = xn_ref[...]
        ys = [_dot(xn, w_ref[:, c:c + MXU_COLS]) for c in range(0, w_ref.shape[1], MXU_COLS)]
        for n, y in enumerate(ys):
            for h in range(0, MXU_COLS, HEAD_DIM):
                yh = y[:, h:h + HEAD_DIM]
                if rope:
                    yh = (yh * cos_ref[...] + pltpu.roll(yh, HEAD_DIM - ROT_DIM // 2, 1) * sa_ref[...]
                          + pltpu.roll(yh, ROT_DIM // 2, 1) * sb_ref[...])
                o_ref[:, n * MXU_COLS + h:n * MXU_COLS + h + HEAD_DIM] = yh.astype(o_ref.dtype)

    pl.when(j < n_rope_tiles)(functools.partial(tile, True))
    pl.when(j >= n_rope_tiles)(functools.partial(tile, False))


def _proj(xn, w, tables, seq, n_rope_cols):
    T, D = xn.shape
    N = w.shape[1]
    tm, tn = PROJ_TM, PROJ_TN
    assert T % tm == 0 and N % tn == 0 and seq % tm == 0 and n_rope_cols % tn == 0 and tn % MXU_COLS == 0
    pos_tiles = seq // tm
    tab_spec = pl.BlockSpec((tm, HEAD_DIM), lambda i, j: (i % pos_tiles, 0))
    return pl.pallas_call(
        functools.partial(_proj_kernel, n_rope_cols // tn),
        grid=(T // tm, N // tn),
        in_specs=[
            pl.BlockSpec((tm, D), lambda i, j: (i, 0)),
            pl.BlockSpec((D, tn), lambda i, j: (0, j)),
            tab_spec, tab_spec, tab_spec,
        ],
        out_specs=pl.BlockSpec((tm, tn), lambda i, j: (i, j)),
        out_shape=jax.ShapeDtypeStruct((T, N), jnp.bfloat16),
        compiler_params=pltpu.CompilerParams(
            dimension_semantics=("parallel", "arbitrary"), vmem_limit_bytes=VMEM_LIMIT),
        name="proj",
    )(xn, w, *tables)


def _rope_tables(seq):
    half = ROT_DIM // 2
    inv_freq = ROPE_THETA ** (-jnp.arange(0, ROT_DIM, 2, dtype=jnp.float32) / ROT_DIM)
    ang = jnp.arange(seq, dtype=jnp.float32)[:, None] * inv_freq[None, :]
    cos, sin = jnp.cos(ang), jnp.sin(ang)
    zeros = jnp.zeros((seq, HEAD_DIM - half), jnp.float32)
    cos_t = jnp.concatenate([cos, cos, jnp.ones((seq, HEAD_DIM - ROT_DIM), jnp.float32)], axis=1)
    sa_t = jnp.concatenate([-sin, zeros], axis=1)
    sb_t = jnp.concatenate([jnp.zeros((seq, half), jnp.float32), sin,
                            jnp.zeros((seq, HEAD_DIM - ROT_DIM), jnp.float32)], axis=1)
    return cos_t, sa_t, sb_t


def _moba_kernel(n_group, q_ref, k_ref, v_ref, o_ref, vt_ref, kmh_ref, kml_ref, pen_ref):
    qi = pl.program_id(2)
    blk = MOBA_BLOCK
    S = k_ref.shape[1]
    nb = S // blk
    nbp = pen_ref.shape[1]
    exp2_scale = HEAD_DIM ** -0.5 * LOG2_E
    heads = [slice(g * HEAD_DIM, (g + 1) * HEAD_DIM) for g in range(n_group)]

    @pl.when(qi == 0)
    def _():
        kmh_ref[...] = jnp.zeros_like(kmh_ref)
        kml_ref[...] = jnp.zeros_like(kml_ref)
        for g, hs in enumerate(heads):
            for c in range(0, S, blk):
                vt_ref[g, :, c:c + blk] = v_ref[0, c:c + blk, hs].astype(jnp.float32).T.astype(vt_ref.dtype)
            for n in range(nb):
                km = jnp.mean(k_ref[0, n * blk:(n + 1) * blk, hs].astype(jnp.float32), axis=0, keepdims=True)
                hi = km.astype(jnp.bfloat16)
                kmh_ref[g, n:n + 1, :] = hi
                kml_ref[g, n:n + 1, :] = (km - hi.astype(jnp.float32)).astype(jnp.bfloat16)

    block_id = lax.broadcasted_iota(jnp.int32, (nbp, blk), 0)
    block_f = block_id.astype(jnp.float32)
    past = block_id < qi
    qts = []
    for g, hs in enumerate(heads):
        qt = q_ref[0, :, hs].astype(jnp.float32).T.astype(jnp.bfloat16)
        gate = _dot(kmh_ref[g], qt) + _dot(kml_ref[g], qt)
        gate = jnp.where(past, gate, -jnp.inf)
        attend = jnp.zeros(gate.shape, jnp.bool_)
        for _ in range(MOBA_TOPK):
            m = jnp.max(gate, axis=0, keepdims=True)
            first = jnp.min(jnp.where(gate == m, block_f, float(nbp)), axis=0, keepdims=True)
            hit = block_f == first
            attend = attend | (hit & past)
            gate = jnp.where(hit, -jnp.inf, gate)
        pen_ref[g] = jnp.where(attend, 0.0, NEG_BIG)
        qts.append(qt)

    def scores(kj, n_blocks):
        start = pl.multiple_of(kj * blk, blk)
        return start, [_dot(k_ref[0, pl.ds(start, n_blocks * blk), hs], qts[g])
                       for g, hs in enumerate(heads)]

    key = lax.broadcasted_iota(jnp.int32, (blk, blk), 0)
    query = lax.broadcasted_iota(jnp.int32, (blk, blk), 1)
    start, ss = scores(qi, 1)
    carry = []
    for g in range(n_group):
        s = jnp.where(key <= query, ss[g], NEG_BIG)
        m0 = jnp.max(s, axis=0, keepdims=True)
        p = jnp.exp2((s - m0) * exp2_scale)
        l0 = jnp.sum(p, axis=0, keepdims=True)
        carry.append((m0, l0, _dot(vt_ref[g, :, pl.ds(start, blk)], p.astype(jnp.bfloat16))))

    def past_blocks(kj, carry, n_blocks):
        start, ss = scores(kj, n_blocks)
        out = []
        for g in range(n_group):
            m_prev, l_prev, acc_prev = carry[g]
            parts = [ss[g][b * blk:(b + 1) * blk] for b in range(n_blocks)]
            pens = [pen_ref[g, pl.ds(kj + b, 1), :] for b in range(n_blocks)]
            m_new = functools.reduce(
                jnp.maximum, [jnp.max(s, axis=0, keepdims=True) + pen for s, pen in zip(parts, pens)], m_prev)
            alpha = jnp.exp2((m_prev - m_new) * exp2_scale)
            p = jnp.concatenate(
                [jnp.exp2((s - (m_new - pen)) * exp2_scale) for s, pen in zip(parts, pens)], axis=0)
            l_new = alpha * l_prev + jnp.sum(p, axis=0, keepdims=True)
            acc_new = alpha * acc_prev + _dot(vt_ref[g, :, pl.ds(start, n_blocks * blk)], p.astype(jnp.bfloat16))
            out.append((m_new, l_new, acc_new))
        return tuple(out)

    carry = lax.fori_loop(0, qi // 2, lambda i, c: past_blocks(2 * i, c, 2), tuple(carry))
    carry = lax.cond(qi % 2 == 1, lambda c: past_blocks(qi - 1, c, 1), lambda c: c, carry)
    for g, hs in enumerate(heads):
        _, l_fin, acc = carry[g]
        o_ref[0, :, hs] = (acc / l_fin).T.astype(o_ref.dtype)


def _moba(qkv, n_heads, q_col, k_col, v_col):
    B, S, _ = qkv.shape
    blk, G = MOBA_BLOCK, MOBA_GROUP
    assert S % blk == 0
    assert n_heads % G == 0 and q_col % G == 0 and k_col % G == 0 and v_col % G == 0
    nbp = -(-(S // blk) // BF16_ROWS) * BF16_ROWS
    W = G * HEAD_DIM
    return pl.pallas_call(
        functools.partial(_moba_kernel, G),
        grid=(B, n_heads // G, S // blk),
        in_specs=[
            pl.BlockSpec((1, blk, W), lambda b, h, i: (b, i, q_col // G + h)),
            pl.BlockSpec((1, S, W), lambda b, h, i: (b, 0, k_col // G + h)),
            pl.BlockSpec((1, S, W), lambda b, h, i: (b, 0, v_col // G + h)),
        ],
        out_specs=pl.BlockSpec((1, blk, W), lambda b, h, i: (b, i, h)),
        out_shape=jax.ShapeDtypeStruct((B, S, n_heads * HEAD_DIM), jnp.bfloat16),
        scratch_shapes=[pltpu.VMEM((G, HEAD_DIM, S), jnp.bfloat16),
                        pltpu.VMEM((G, nbp, HEAD_DIM), jnp.bfloat16),
                        pltpu.VMEM((G, nbp, HEAD_DIM), jnp.bfloat16),
                        pltpu.VMEM((G, nbp, blk), jnp.float32)],
        compiler_params=pltpu.CompilerParams(
            dimension_semantics=("parallel", "parallel", "arbitrary")),
        name="moba",
    )(qkv, qkv, qkv)


def _sb_kernel(n_group, q_ref, k_ref, v_ref, tri_ref, o_ref, vt_ref):
    qi = pl.program_id(2)
    blk = SB_BLOCK
    S = k_ref.shape[1]
    scale = HEAD_DIM ** -0.5
    neg_tri = tri_ref[...]
    heads = [slice(g * HEAD_DIM, (g + 1) * HEAD_DIM) for g in range(n_group)]
    qts = [q_ref[0, :, hs].astype(jnp.float32).T.astype(jnp.bfloat16) for hs in heads]

    @pl.when(qi == 0)
    def _():
        for g, hs in enumerate(heads):
            for c in range(0, S, blk):
                vt_ref[g, :, c:c + blk] = v_ref[0, c:c + blk, hs].astype(jnp.float32).T.astype(vt_ref.dtype)

    def block(kj, carry, diagonal):
        start = pl.multiple_of(kj * blk, blk)
        if diagonal:
            key = lax.broadcasted_iota(jnp.int32, (blk, blk), 0)
            query = lax.broadcasted_iota(jnp.int32, (blk, blk), 1)
            causal = key < query
        zs = [_dot(k_ref[0, pl.ds(start, blk), hs], qts[g]) * scale for g, hs in enumerate(heads)]
        log_betas, col_sums, sticks = [], [], []
        for g in range(n_group):
            z = zs[g]
            softplus = jnp.maximum(z, 0.0) + jnp.log(1.0 + jnp.exp2(jnp.abs(z) * -LOG2_E))
            log_betas.append(z - softplus)
            if diagonal:
                softplus = jnp.where(causal, softplus, 0.0)
            col_sums.append(jnp.sum(softplus, axis=0, keepdims=True))
            sticks.append(_dot(neg_tri, softplus.astype(jnp.bfloat16)))
        out = []
        for g in range(n_group):
            rest, acc = carry[g]
            a = jnp.exp(log_betas[g] + sticks[g] + rest)
            if diagonal:
                a = jnp.where(causal, a, 0.0)
            acc = acc + _dot(vt_ref[g, :, pl.ds(start, blk)], a.astype(jnp.bfloat16))
            out.append((rest - col_sums[g], acc))
        return tuple(out)

    init = tuple((jnp.zeros((1, blk), jnp.float32), jnp.zeros((HEAD_DIM, blk), jnp.float32))
                 for _ in heads)
    def any_stick_left(carry):
        rest_max = functools.reduce(jnp.maximum, [rest for rest, _ in carry])
        return (jnp.max(rest_max) > SB_DEAD_LOG).astype(jnp.int32)

    def older_block(state):
        kj, _, carry = state
        carry = block(kj, carry, False)
        return kj - 1, any_stick_left(carry), carry

    carry = block(qi, init, True)
    _, _, carry = lax.while_loop(lambda s: (s[0] >= 0) & (s[1] > 0), older_block,
                                 (qi - 1, any_stick_left(carry), carry))
    for g, hs in enumerate(heads):
        o_ref[0, :, hs] = carry[g][1].T.astype(o_ref.dtype)


def _sb(qkv, n_heads, q_col, k_col, v_col):
    B, S, _ = qkv.shape
    blk, G = SB_BLOCK, SB_GROUP
    assert S % blk == 0 and n_heads % G == 0 and q_col % G == 0 and k_col % G == 0 and v_col % G == 0
    idx = jnp.arange(blk)
    neg_tri = -(idx[None, :] > idx[:, None]).astype(jnp.bfloat16)
    W = G * HEAD_DIM
    return pl.pallas_call(
        functools.partial(_sb_kernel, G),
        grid=(B, n_heads // G, S // blk),
        in_specs=[
            pl.BlockSpec((1, blk, W), lambda b, h, i: (b, i, q_col // G + h)),
            pl.BlockSpec((1, S, W), lambda b, h, i: (b, 0, k_col // G + h)),
            pl.BlockSpec((1, S, W), lambda b, h, i: (b, 0, v_col // G + h)),
            pl.BlockSpec((blk, blk), lambda b, h, i: (0, 0)),
        ],
        out_specs=pl.BlockSpec((1, blk, W), lambda b, h, i: (b, i, h)),
        out_shape=jax.ShapeDtypeStruct((B, S, n_heads * HEAD_DIM), jnp.bfloat16),
        scratch_shapes=[pltpu.VMEM((G, HEAD_DIM, S), jnp.bfloat16)],
        compiler_params=pltpu.CompilerParams(
            dimension_semantics=("parallel", "parallel", "arbitrary")),
        name="sb",
    )(qkv, qkv, qkv, neg_tri)


def _out_kernel(x_ref, oa_ref, ob_ref, ga_ref, gb_ref, wa_ref, wb_ref, g_ref, next_g_ref, o_ref, on_ref):
    half = x_ref.shape[0] // 2
    rows = [slice(0, half), slice(half, 2 * half)]
    ns = [(_rms(oa_ref[rs, :].astype(jnp.float32), ga_ref[...]).astype(jnp.bfloat16),
           _rms(ob_ref[rs, :].astype(jnp.float32), gb_ref[...]).astype(jnp.bfloat16)) for rs in rows]
    fs = [_dot(na, wa_ref[...]) + _dot(nb, wb_ref[...]) for na, nb in ns]
    for rs, f in zip(rows, fs):
        y = x_ref[rs, :] + _rms(f, g_ref[...])
        o_ref[rs, :] = y
        on_ref[rs, :] = _rms(y, next_g_ref[...]).astype(on_ref.dtype)


def _out(x, oa, ob, ga, gb, w, g, next_g):
    T, D = x.shape
    Wa, Wb = oa.shape[1], ob.shape[1]
    assert Wa == Wb and w.shape[0] == Wa + Wb
    tm = OUT_TM
    assert T % tm == 0
    return pl.pallas_call(
        _out_kernel,
        grid=(T // tm,),
        in_specs=[
            pl.BlockSpec((tm, D), lambda i: (i, 0)),
            pl.BlockSpec((tm, Wa), lambda i: (i, 0)),
            pl.BlockSpec((tm, Wb), lambda i: (i, 0)),
            pl.BlockSpec((1, Wa), lambda i: (0, 0)),
            pl.BlockSpec((1, Wb), lambda i: (0, 0)),
            pl.BlockSpec((Wa, D), lambda i: (0, 0)),
            pl.BlockSpec((Wb, D), lambda i: (1, 0)),
            pl.BlockSpec((1, D), lambda i: (0, 0)),
            pl.BlockSpec((1, D), lambda i: (0, 0)),
        ],
        out_specs=[pl.BlockSpec((tm, D), lambda i: (i, 0)), pl.BlockSpec((tm, D), lambda i: (i, 0))],
        out_shape=[jax.ShapeDtypeStruct((T, D), jnp.float32), jax.ShapeDtypeStruct((T, D), jnp.bfloat16)],
        compiler_params=pltpu.CompilerParams(
            dimension_semantics=("parallel",), vmem_limit_bytes=VMEM_LIMIT),
        name="out_proj",
    )(x, oa, ob, ga, gb, w, w, g, next_g)


def kernel(x, ffn1_pre_g, ffn1_w_gate, ffn1_w_up, ffn1_w_down, ffn1_post_g, mix_pre_g, w_in, moba_out_g, sb_out_g, w_out, mix_post_g, ffn2_pre_g, ffn2_w_gate, ffn2_w_up, ffn2_w_down, ffn2_post_g):
    B, S, D = x.shape
    depth = w_in.shape[0]
    w_moba = moba_out_g.shape[1]
    w_sb = sb_out_g.shape[1]
    h_moba, h_sb = w_moba // HEAD_DIM, w_sb // HEAD_DIM
    tables = _rope_tables(S)

    xt = x.reshape(B * S, D)
    for l in range(depth):
        h, (wd1,) = _ffn_up(_prenorm(xt, ffn1_pre_g[l:l + 1]), ffn1_w_gate[l], ffn1_w_up[l],
                            [ffn1_w_down[l]])
        xt, xn, (w_in_bf, w_out_bf) = _ffn_down(h, wd1, xt, ffn1_post_g[l:l + 1], mix_pre_g[l:l + 1],
                                                [w_in[l], w_out[l]])
        qkv = _proj(xn, w_in_bf, tables, S, 2 * w_moba).reshape(B, S, -1)
        o_a = _moba(qkv, h_moba, 0, h_moba, 2 * h_moba)
        o_b = _sb(qkv, h_sb, 3 * h_moba, 3 * h_moba + h_sb, 3 * h_moba + 2 * h_sb)
        xt, xn = _out(xt, o_a.reshape(B * S, w_moba), o_b.reshape(B * S, w_sb),
                      moba_out_g[l:l + 1], sb_out_g[l:l + 1], w_out_bf, mix_post_g[l:l + 1],
                      ffn2_pre_g[l:l + 1])
        h, (wd2,) = _ffn_up(xn, ffn2_w_gate[l], ffn2_w_up[l], [ffn2_w_down[l]])
        xt, _, _ = _ffn_down(h, wd2, xt, ffn2_post_g[l:l + 1], None, [])
    return xt.reshape(B, S, D)
```

```python
import functools

import jax
import jax.numpy as jnp
from jax import lax
from jax.experimental import pallas as pl
from jax.experimental.pallas import tpu as pltpu

HEAD_DIM = 128
MOBA_BLOCK = 256
MOBA_TOPK = 3
ROPE_THETA = 500000.0
ROT_DIM = HEAD_DIM // 4
RMS_EPS = 1e-6
FFN_RES_SCALE = 0.5

BF16_ROWS = 16
MXU_COLS = 256
VMEM_LIMIT = 56 * 1024 * 1024
LOG2_E = 1.4426950408889634
NEG_BIG = -1e30
SB_BLOCK = 256
SB_DEAD_LOG = -104.0

NORM_TM = 1024
UP_TM, UP_TF = 2048, 512
DOWN_TM = 256
PROJ_TM, PROJ_TN = 1024, 2048
OUT_TM = 512


def _rms(xf, g):
    return xf * lax.rsqrt(jnp.mean(xf * xf, axis=-1, keepdims=True) + RMS_EPS) * g


def _dot(a, b):
    return jnp.dot(a, b, preferred_element_type=jnp.float32)


def _cast_specs(weights, n_steps, step_index):
    specs, shapes = [], []
    for w in weights:
        rows, cols = w.shape
        assert rows % (n_steps * BF16_ROWS) == 0
        specs.append(pl.BlockSpec((rows // n_steps, cols), step_index))
        shapes.append(jax.ShapeDtypeStruct(w.shape, jnp.bfloat16))
    return specs, shapes


def _cast_blocks(src_refs, dst_refs):
    for src, dst in zip(src_refs, dst_refs):
        dst[...] = src[...].astype(dst.dtype)


def _prenorm_kernel(x_ref, g_ref, o_ref):
    o_ref[...] = _rms(x_ref[...], g_ref[...]).astype(o_ref.dtype)


def _prenorm(x, g):
    T, D = x.shape
    tm = NORM_TM
    assert T % tm == 0
    return pl.pallas_call(
        _prenorm_kernel,
        grid=(T // tm,),
        in_specs=[pl.BlockSpec((tm, D), lambda i: (i, 0)), pl.BlockSpec((1, D), lambda i: (0, 0))],
        out_specs=pl.BlockSpec((tm, D), lambda i: (i, 0)),
        out_shape=jax.ShapeDtypeStruct((T, D), jnp.bfloat16),
        compiler_params=pltpu.CompilerParams(
            dimension_semantics=("parallel",), vmem_limit_bytes=VMEM_LIMIT),
        name="prenorm",
    )(x, g)


def _ffn_up_kernel(n_cast, xn_ref, wg_ref, wu_ref, *refs):
    h_ref = refs[n_cast]
    _cast_blocks(refs[:n_cast], refs[n_cast + 1:])
    xn = xn_ref[...]
    cols = [slice(c, c + MXU_COLS) for c in range(0, h_ref.shape[1], MXU_COLS)]
    gu = [(_dot(xn, wg_ref[:, cs].astype(jnp.bfloat16)), _dot(xn, wu_ref[:, cs].astype(jnp.bfloat16)))
          for cs in cols]
    for cs, (g, u) in zip(cols, gu):
        h_ref[:, cs] = (g * jax.nn.sigmoid(g) * u).astype(h_ref.dtype)


def _ffn_up(xn, wg, wu, to_cast):
    T, D = xn.shape
    F = wg.shape[1]
    tm, tf = UP_TM, UP_TF
    assert T % tm == 0 and F % tf == 0 and tf % MXU_COLS == 0
    n_i, n_j = T // tm, F // tf
    w_spec = pl.BlockSpec((D, tf), lambda i, j: (0, j))
    cast_specs, cast_shapes = _cast_specs(to_cast, n_i * n_j, lambda i, j: (i * n_j + j, 0))
    outs = pl.pallas_call(
        functools.partial(_ffn_up_kernel, len(to_cast)),
        grid=(n_i, n_j),
        in_specs=[pl.BlockSpec((tm, D), lambda i, j: (i, 0)), w_spec, w_spec] + cast_specs,
        out_specs=[pl.BlockSpec((tm, tf), lambda i, j: (i, j))] + cast_specs,
        out_shape=[jax.ShapeDtypeStruct((T, F), jnp.bfloat16)] + cast_shapes,
        compiler_params=pltpu.CompilerParams(
            dimension_semantics=("arbitrary", "arbitrary"), vmem_limit_bytes=VMEM_LIMIT),
        name="ffn_up",
    )(xn, wg, wu, *to_cast)
    return outs[0], outs[1:]


def _ffn_down_kernel(emit_next, n_cast, h_ref, wd_ref, x_ref, post_g_ref, *refs):
    n_in = n_cast + emit_next
    n_out = 1 + emit_next
    _cast_blocks(refs[emit_next:n_in], refs[n_in + n_out:])
    o_ref = refs[n_in]
    y = x_ref[...] + FFN_RES_SCALE * _rms(_dot(h_ref[...], wd_ref[...]), post_g_ref[...])
    if emit_next:
        next_g_ref, on_ref = refs[0], refs[n_in + 1]
        on_ref[...] = _rms(y, next_g_ref[...]).astype(on_ref.dtype)
    o_ref[...] = y


def _ffn_down(h, wd, x, post_g, next_g, to_cast):
    T, F = h.shape
    D = wd.shape[1]
    tm = DOWN_TM
    assert T % tm == 0
    emit_next = next_g is not None
    row_spec = pl.BlockSpec((tm, D), lambda i: (i, 0))
    gain_spec = pl.BlockSpec((1, D), lambda i: (0, 0))
    cast_specs, cast_shapes = _cast_specs(to_cast, T // tm, lambda i: (i, 0))
    in_specs = [pl.BlockSpec((tm, F), lambda i: (i, 0)),
                pl.BlockSpec((F, D), lambda i: (0, 0), pipeline_mode=pl.Buffered(1)),
                row_spec, gain_spec]
    args = [h, wd, x, post_g]
    out_specs, out_shape = [row_spec], [jax.ShapeDtypeStruct((T, D), jnp.float32)]
    if emit_next:
        in_specs.append(gain_spec)
        args.append(next_g)
        out_specs.append(row_spec)
        out_shape.append(jax.ShapeDtypeStruct((T, D), jnp.bfloat16))
    outs = pl.pallas_call(
        functools.partial(_ffn_down_kernel, emit_next, len(to_cast)),
        grid=(T // tm,),
        in_specs=in_specs + cast_specs,
        out_specs=out_specs + cast_specs,
        out_shape=out_shape + cast_shapes,
        compiler_params=pltpu.CompilerParams(
            dimension_semantics=("arbitrary",), vmem_limit_bytes=VMEM_LIMIT),
        name="ffn_down",
    )(*args, *to_cast)
    n_out = 1 + emit_next
    return outs[0], (outs[1] if emit_next else None), outs[n_out:]


def _proj_kernel(n_rope_tiles, xn_ref, w_ref, cos_ref, sa_ref, sb_ref, o_ref):
    j = pl.program_id(1)

    def tile(rope):
        xn = xn_ref[...]
        ys = [_dot(xn, w_ref[:, c:c + MXU_COLS]) for c in range(0, w_ref.shape[1], MXU_COLS)]
        for n, y in enumerate(ys):
            for h in range(0, MXU_COLS, HEAD_DIM):
                yh = y[:, h:h + HEAD_DIM]
                if rope:
                    yh = (yh * cos_ref[...] + pltpu.roll(yh, HEAD_DIM - ROT_DIM // 2, 1) * sa_ref[...]
                          + pltpu.roll(yh, ROT_DIM // 2, 1) * sb_ref[...])
                o_ref[:, n * MXU_COLS + h:n * MXU_COLS + h + HEAD_DIM] = yh.astype(o_ref.dtype)

    pl.when(j < n_rope_tiles)(functools.partial(tile, True))
    pl.when(j >= n_rope_tiles)(functools.partial(tile, False))


def _proj(xn, w, tables, seq, n_rope_cols):
    T, D = xn.shape
    N = w.shape[1]
    tm, tn = PROJ_TM, PROJ_TN
    assert T % tm == 0 and N % tn == 0 and seq % tm == 0 and n_rope_cols % tn == 0 and tn % MXU_COLS == 0
    pos_tiles = seq // tm
    tab_spec = pl.BlockSpec((tm, HEAD_DIM), lambda i, j: (i % pos_tiles, 0))
    return pl.pallas_call(
        functools.partial(_proj_kernel, n_rope_cols // tn),
        grid=(T // tm, N // tn),
        in_specs=[
            pl.BlockSpec((tm, D), lambda i, j: (i, 0)),
            pl.BlockSpec((D, tn), lambda i, j: (0, j)),
            tab_spec, tab_spec, tab_spec,
        ],
        out_specs=pl.BlockSpec((tm, tn), lambda i, j: (i, j)),
        out_shape=jax.ShapeDtypeStruct((T, N), jnp.bfloat16),
        compiler_params=pltpu.CompilerParams(
            dimension_semantics=("parallel", "arbitrary"), vmem_limit_bytes=VMEM_LIMIT),
        name="proj",
    )(xn, w, *tables)


def _rope_tables(seq):
    half = ROT_DIM // 2
    inv_freq = ROPE_THETA ** (-jnp.arange(0, ROT_DIM, 2, dtype=jnp.float32) / ROT_DIM)
    ang = jnp.arange(seq, dtype=jnp.float32)[:, None] * inv_freq[None, :]
    cos, sin = jnp.cos(ang), jnp.sin(ang)
    zeros = jnp.zeros((seq, HEAD_DIM - half), jnp.float32)
    cos_t = jnp.concatenate([cos, cos, jnp.ones((seq, HEAD_DIM - ROT_DIM), jnp.float32)], axis=1)
    sa_t = jnp.concatenate([-sin, zeros], axis=1)
    sb_t = jnp.concatenate([jnp.zeros((seq, half), jnp.float32), sin,
                            jnp.zeros((seq, HEAD_DIM - ROT_DIM), jnp.float32)], axis=1)
    return cos_t, sa_t, sb_t


def _transpose_values(v_ref, vt_ref, heads, blk):
    for g, hs in enumerate(heads):
        for c in range(0, v_ref.shape[1], blk):
            vt_ref[g, :, c:c + blk] = v_ref[0, c:c + blk, hs].astype(jnp.float32).T.astype(vt_ref.dtype)


def _moba_phases(qi, heads, q_ref, k_ref, v_ref, o_ref, vt_ref, kmh_ref, kml_ref, pen_ref):
    blk = MOBA_BLOCK
    S = k_ref.shape[1]
    nb = S // blk
    nbp = pen_ref.shape[1]
    n_group = len(heads)
    exp2_scale = HEAD_DIM ** -0.5 * LOG2_E
    qts = []

    def setup():
        kmh_ref[...] = jnp.zeros_like(kmh_ref)
        kml_ref[...] = jnp.zeros_like(kml_ref)
        _transpose_values(v_ref, vt_ref, heads, blk)
        for g, hs in enumerate(heads):
            for n in range(nb):
                km = jnp.mean(k_ref[0, n * blk:(n + 1) * blk, hs].astype(jnp.float32), axis=0, keepdims=True)
                hi = km.astype(jnp.bfloat16)
                kmh_ref[g, n:n + 1, :] = hi
                kml_ref[g, n:n + 1, :] = (km - hi.astype(jnp.float32)).astype(jnp.bfloat16)

    def scores(kj, n_blocks):
        start = pl.multiple_of(kj * blk, blk)
        return start, [_dot(k_ref[0, pl.ds(start, n_blocks * blk), hs], qts[g])
                       for g, hs in enumerate(heads)]

    def first():
        block_id = lax.broadcasted_iota(jnp.int32, (nbp, blk), 0)
        block_f = block_id.astype(jnp.float32)
        past = block_id < qi
        for g, hs in enumerate(heads):
            qt = q_ref[0, :, hs].astype(jnp.float32).T.astype(jnp.bfloat16)
            gate = _dot(kmh_ref[g], qt) + _dot(kml_ref[g], qt)
            gate = jnp.where(past, gate, -jnp.inf)
            attend = jnp.zeros(gate.shape, jnp.bool_)
            for _ in range(MOBA_TOPK):
                m = jnp.max(gate, axis=0, keepdims=True)
                first_id = jnp.min(jnp.where(gate == m, block_f, float(nbp)), axis=0, keepdims=True)
                hit = block_f == first_id
                attend = attend | (hit & past)
                gate = jnp.where(hit, -jnp.inf, gate)
            pen_ref[g] = jnp.where(attend, 0.0, NEG_BIG)
            qts.append(qt)
        key = lax.broadcasted_iota(jnp.int32, (blk, blk), 0)
        query = lax.broadcasted_iota(jnp.int32, (blk, blk), 1)
        start, ss = scores(qi, 1)
        carry = []
        for g in range(n_group):
            s = jnp.where(key <= query, ss[g], NEG_BIG)
            m0 = jnp.max(s, axis=0, keepdims=True)
            p = jnp.exp2((s - m0) * exp2_scale)
            l0 = jnp.sum(p, axis=0, keepdims=True)
            carry.append((m0, l0, _dot(vt_ref[g, :, pl.ds(start, blk)], p.astype(jnp.bfloat16))))
        return tuple(carry)

    def past_blocks(kj, carry, n_blocks):
        start, ss = scores(kj, n_blocks)
        out = []
        for g in range(n_group):
            m_prev, l_prev, acc_prev = carry[g]
            parts = [ss[g][b * blk:(b + 1) * blk] for b in range(n_blocks)]
            pens = [pen_ref[g, pl.ds(kj + b, 1), :] for b in range(n_blocks)]
            m_new = functools.reduce(
                jnp.maximum, [jnp.max(s, axis=0, keepdims=True) + pen for s, pen in zip(parts, pens)], m_prev)
            alpha = jnp.exp2((m_prev - m_new) * exp2_scale)
            p = jnp.concatenate(
                [jnp.exp2((s - (m_new - pen)) * exp2_scale) for s, pen in zip(parts, pens)], axis=0)
            l_new = alpha * l_prev + jnp.sum(p, axis=0, keepdims=True)
            acc_new = alpha * acc_prev + _dot(vt_ref[g, :, pl.ds(start, n_blocks * blk)], p.astype(jnp.bfloat16))
            out.append((m_new, l_new, acc_new))
        return tuple(out)

    def older(carry):
        carry = lax.fori_loop(0, qi // 2, lambda i, c: past_blocks(2 * i, c, 2), carry)
        return lax.cond(qi % 2 == 1, lambda c: past_blocks(qi - 1, c, 1), lambda c: c, carry)

    def store(carry):
        for g, hs in enumerate(heads):
            _, l_fin, acc = carry[g]
            o_ref[0, :, hs] = (acc / l_fin).T.astype(o_ref.dtype)

    return setup, first, older, store


def _sb_phases(qi, heads, q_ref, k_ref, v_ref, tri_ref, o_ref, vt_ref):
    blk = SB_BLOCK
    n_group = len(heads)
    scale = HEAD_DIM ** -0.5
    qts = []

    def setup():
        _transpose_values(v_ref, vt_ref, heads, blk)

    def block(kj, carry, diagonal):
        neg_tri = tri_ref[...]
        start = pl.multiple_of(kj * blk, blk)
        if diagonal:
            key = lax.broadcasted_iota(jnp.int32, (blk, blk), 0)
            query = lax.broadcasted_iota(jnp.int32, (blk, blk), 1)
            causal = key < query
        zs = [_dot(k_ref[0, pl.ds(start, blk), hs], qts[g]) * scale for g, hs in enumerate(heads)]
        log_betas, col_sums, sticks = [], [], []
        for g in range(n_group):
            z = zs[g]
            softplus = jnp.maximum(z, 0.0) + jnp.log(1.0 + jnp.exp2(jnp.abs(z) * -LOG2_E))
            log_betas.append(z - softplus)
            if diagonal:
                softplus = jnp.where(causal, softplus, 0.0)
            col_sums.append(jnp.sum(softplus, axis=0, keepdims=True))
            sticks.append(_dot(neg_tri, softplus.astype(jnp.bfloat16)))
        out = []
        for g in range(n_group):
            rest, acc = carry[g]
            a = jnp.exp(log_betas[g] + sticks[g] + rest)
            if diagonal:
                a = jnp.where(causal, a, 0.0)
            acc = acc + _dot(vt_ref[g, :, pl.ds(start, blk)], a.astype(jnp.bfloat16))
            out.append((rest - col_sums[g], acc))
        return tuple(out)

    def first():
        qts.extend(q_ref[0, :, hs].astype(jnp.float32).T.astype(jnp.bfloat16) for hs in heads)
        init = tuple((jnp.zeros((1, blk), jnp.float32), jnp.zeros((HEAD_DIM, blk), jnp.float32))
                     for _ in heads)
        return block(qi, init, True)

    def any_stick_left(carry):
        rest_max = functools.reduce(jnp.maximum, [rest for rest, _ in carry])
        return (jnp.max(rest_max) > SB_DEAD_LOG).astype(jnp.int32)

    def older_block(state):
        kj, _, carry = state
        carry = block(kj, carry, False)
        return kj - 1, any_stick_left(carry), carry

    def older(carry):
        return lax.while_loop(lambda s: (s[0] >= 0) & (s[1] > 0), older_block,
                              (qi - 1, any_stick_left(carry), carry))[2]

    def store(carry):
        for g, hs in enumerate(heads):
            o_ref[0, :, hs] = carry[g][1].T.astype(o_ref.dtype)

    return setup, first, older, store


def _mixer_kernel(n_a, n_b, qa_ref, ka_ref, va_ref, qb_ref, kb_ref, vb_ref, tri_ref, oa_ref, ob_ref,
                  vta_ref, kmh_ref, kml_ref, pen_ref, vtb_ref):
    qi = pl.program_id(1)
    lanes = lambda n: [slice(g * HEAD_DIM, (g + 1) * HEAD_DIM) for g in range(n)]
    moba = _moba_phases(qi, lanes(n_a), qa_ref, ka_ref, va_ref, oa_ref, vta_ref, kmh_ref, kml_ref, pen_ref)
    sb = _sb_phases(qi, lanes(n_b), qb_ref, kb_ref, vb_ref, tri_ref, ob_ref, vtb_ref)

    @pl.when(qi == 0)
    def _():
        moba[0]()
        sb[0]()

    carry_a, carry_b = moba[1](), sb[1]()
    carry_a = moba[2](carry_a)
    carry_b = sb[2](carry_b)
    moba[3](carry_a)
    sb[3](carry_b)


def _mixers(qkv, n_a, n_b):
    B, S, _ = qkv.shape
    blk = MOBA_BLOCK
    assert SB_BLOCK == blk and S % blk == 0 and n_a == n_b
    nbp = -(-(S // blk) // BF16_ROWS) * BF16_ROWS
    W = n_a * HEAD_DIM
    idx = jnp.arange(blk)
    neg_tri = -(idx[None, :] > idx[:, None]).astype(jnp.bfloat16)
    q_spec = lambda c: pl.BlockSpec((1, blk, W), lambda b, i: (b, i, c))
    kv_spec = lambda c: pl.BlockSpec((1, S, W), lambda b, i: (b, 0, c))
    o_spec = pl.BlockSpec((1, blk, W), lambda b, i: (b, i, 0))
    o_shape = jax.ShapeDtypeStruct((B, S, W), jnp.bfloat16)
    return pl.pallas_call(
        functools.partial(_mixer_kernel, n_a, n_b),
        grid=(B, S // blk),
        in_specs=[q_spec(0), kv_spec(1), kv_spec(2), q_spec(3), kv_spec(4), kv_spec(5),
                  pl.BlockSpec((blk, blk), lambda b, i: (0, 0))],
        out_specs=[o_spec, o_spec],
        out_shape=[o_shape, o_shape],
        scratch_shapes=[pltpu.VMEM((n_a, HEAD_DIM, S), jnp.bfloat16),
                        pltpu.VMEM((n_a, nbp, HEAD_DIM), jnp.bfloat16),
                        pltpu.VMEM((n_a, nbp, HEAD_DIM), jnp.bfloat16),
                        pltpu.VMEM((n_a, nbp, blk), jnp.float32),
                        pltpu.VMEM((n_b, HEAD_DIM, S), jnp.bfloat16)],
        compiler_params=pltpu.CompilerParams(
            dimension_semantics=("parallel", "arbitrary"), vmem_limit_bytes=VMEM_LIMIT),
        name="mixers",
    )(qkv, qkv, qkv, qkv, qkv, qkv, neg_tri)


def _out_kernel(x_ref, oa_ref, ob_ref, ga_ref, gb_ref, wa_ref, wb_ref, g_ref, next_g_ref, o_ref, on_ref):
    half = x_ref.shape[0] // 2
    rows = [slice(0, half), slice(half, 2 * half)]
    ns = [(_rms(oa_ref[rs, :].astype(jnp.float32), ga_ref[...]).astype(jnp.bfloat16),
           _rms(ob_ref[rs, :].astype(jnp.float32), gb_ref[...]).astype(jnp.bfloat16)) for rs in rows]
    fs = [_dot(na, wa_ref[...]) + _dot(nb, wb_ref[...]) for na, nb in ns]
    for rs, f in zip(rows, fs):
        y = x_ref[rs, :] + _rms(f, g_ref[...])
        o_ref[rs, :] = y
        on_ref[rs, :] = _rms(y, next_g_ref[...]).astype(on_ref.dtype)


def _out(x, oa, ob, ga, gb, w, g, next_g):
    T, D = x.shape
    Wa, Wb = oa.shape[1], ob.shape[1]
    assert Wa == Wb and w.shape[0] == Wa + Wb
    tm = OUT_TM
    assert T % tm == 0
    return pl.pallas_call(
        _out_kernel,
        grid=(T // tm,),
        in_specs=[
            pl.BlockSpec((tm, D), lambda i: (i, 0)),
            pl.BlockSpec((tm, Wa), lambda i: (i, 0)),
            pl.BlockSpec((tm, Wb), lambda i: (i, 0)),
            pl.BlockSpec((1, Wa), lambda i: (0, 0)),
            pl.BlockSpec((1, Wb), lambda i: (0, 0)),
            pl.BlockSpec((Wa, D), lambda i: (0, 0)),
            pl.BlockSpec((Wb, D), lambda i: (1, 0)),
            pl.BlockSpec((1, D), lambda i: (0, 0)),
            pl.BlockSpec((1, D), lambda i: (0, 0)),
        ],
        out_specs=[pl.BlockSpec((tm, D), lambda i: (i, 0)), pl.BlockSpec((tm, D), lambda i: (i, 0))],
        out_shape=[jax.ShapeDtypeStruct((T, D), jnp.float32), jax.ShapeDtypeStruct((T, D), jnp.bfloat16)],
        compiler_params=pltpu.CompilerParams(
            dimension_semantics=("parallel",), vmem_limit_bytes=VMEM_LIMIT),
        name="out_proj",
    )(x, oa, ob, ga, gb, w, w, g, next_g)


def kernel(x, ffn1_pre_g, ffn1_w_gate, ffn1_w_up, ffn1_w_down, ffn1_post_g, mix_pre_g, w_in, moba_out_g, sb_out_g, w_out, mix_post_g, ffn2_pre_g, ffn2_w_gate, ffn2_w_up, ffn2_w_down, ffn2_post_g):
    B, S, D = x.shape
    depth = w_in.shape[0]
    w_moba = moba_out_g.shape[1]
    w_sb = sb_out_g.shape[1]
    h_moba, h_sb = w_moba // HEAD_DIM, w_sb // HEAD_DIM
    tables = _rope_tables(S)

    xt = x.reshape(B * S, D)
    for l in range(depth):
        h, (wd1,) = _ffn_up(_prenorm(xt, ffn1_pre_g[l:l + 1]), ffn1_w_gate[l], ffn1_w_up[l],
                            [ffn1_w_down[l]])
        xt, xn, (w_in_bf, w_out_bf) = _ffn_down(h, wd1, xt, ffn1_post_g[l:l + 1], mix_pre_g[l:l + 1],
                                                [w_in[l], w_out[l]])
        qkv = _proj(xn, w_in_bf, tables, S, 2 * w_moba).reshape(B, S, -1)
        o_a, o_b = _mixers(qkv, h_moba, h_sb)
        xt, xn = _out(xt, o_a.reshape(B * S, w_moba), o_b.reshape(B * S, w_sb),
                      moba_out_g[l:l + 1], sb_out_g[l:l + 1], w_out_bf, mix_post_g[l:l + 1],
                      ffn2_pre_g[l:l + 1])
        h, (wd2,) = _ffn_up(xn, ffn2_w_gate[l], ffn2_w_up[l], [ffn2_w_down[l]])
        xt, _, _ = _ffn_down(h, wd2, xt, ffn2_post_g[l:l + 1], None, [])
    return xt.reshape(B, S, D)
```
